```python
import jax, jax.numpy as jnp
from jax import lax
import numpy as np

D_MODEL = 1024
BATCH = 16
SEQ = 2048
DEPTH = 4
DEC_BATCH = 32
DEC_SEQ = 32
PAST_LEN = 2048

CHUNK = 64
N_HEADS = 16
N_KV_HEADS = 4
HEAD_DIM = 64
GROUP = N_HEADS // N_KV_HEADS
WINDOW = 128
N_BACK_CHUNKS = WINDOW // CHUNK
D_ATTN = N_HEADS * HEAD_DIM
D_KV = N_KV_HEADS * HEAD_DIM
D_LRU = D_MODEL
N_LRU_BLOCKS = 16
LRU_BLOCK = D_LRU // N_LRU_BLOCKS
LRU_C = 8.0
LRU_CONV_W = 4
D_FF = 3 * D_MODEL
FFN_CONV_W = 3
N_BRANCH = 2
D_IN = 2 * D_LRU + D_ATTN + 2 * D_KV + N_BRANCH * D_MODEL
IN_SPLITS = [D_LRU, 2 * D_LRU, 2 * D_LRU + D_ATTN, 2 * D_LRU + D_ATTN + D_KV, 2 * D_LRU + D_ATTN + 2 * D_KV]
EPS = 1e-6
NEG = -1e30

kernel_name = "hawk_swa_sink_convffn_stream"


def rmsnorm(x, g):
    xf = x.astype(jnp.float32)
    y = xf * lax.rsqrt(jnp.mean(xf * xf, axis=-1, keepdims=True) + EPS)
    return (y * g.astype(jnp.float32)).astype(x.dtype)


def causal_dwconv(x, prev, w, b):
    width = w.shape[0]
    T = x.shape[1]
    xe = jnp.concatenate([prev.astype(x.dtype), x], axis=1)
    y = b
    for j in range(width):
        y = y + w[j] * xe[:, j:j + T]
    return y, xe[:, xe.shape[1] - (width - 1):]


def block_diag(x, w, b):
    B, T, _ = x.shape
    xr = x.reshape(B, T, N_LRU_BLOCKS, LRU_BLOCK)
    return jnp.einsum('btnc,ncd->btnd', xr, w).reshape(B, T, D_LRU) + b


def rg_lru(x, h_prev, w_a, b_a, w_i, b_i, lam):
    r = jax.nn.sigmoid(block_diag(x, w_a, b_a).astype(jnp.float32))
    i = jax.nn.sigmoid(block_diag(x, w_i, b_i).astype(jnp.float32))
    log_a = -LRU_C * r * jax.nn.softplus(-lam.astype(jnp.float32))
    a = jnp.exp(log_a)
    mult = jnp.sqrt(-jnp.expm1(2.0 * log_a))
    bterm = mult * i * x.astype(jnp.float32)

    def combine(lhs, rhs):
        a1, b1 = lhs
        a2, b2 = rhs
        return a1 * a2, a2 * b1 + b2

    A, Bc = lax.associative_scan(combine, (a, bterm), axis=1)
    h = A * h_prev.astype(jnp.float32)[:, None] + Bc
    return h.astype(x.dtype), h[:, -1].astype(h_prev.dtype)


def sink_softmax(s, sink):
    col = jnp.broadcast_to(sink, s.shape[:-1] + (1,))
    return jax.nn.softmax(jnp.concatenate([s, col], axis=-1), axis=-1)[..., :-1]


def attn_prompt(q, k, v, sink):
    B, S = q.shape[:2]
    nc = S // CHUNK
    nkb = N_BACK_CHUNKS + 1
    qc = q.reshape(B, nc, CHUNK, N_KV_HEADS, GROUP, HEAD_DIM)
    pad = ((0, 0), (N_BACK_CHUNKS * CHUNK, 0), (0, 0), (0, 0))
    kp = jnp.pad(k, pad).reshape(B, nc + N_BACK_CHUNKS, CHUNK, N_KV_HEADS, HEAD_DIM)
    vp = jnp.pad(v, pad).reshape(B, nc + N_BACK_CHUNKS, CHUNK, N_KV_HEADS, HEAD_DIM)
    kb = jnp.concatenate([kp[:, j:j + nc] for j in range(nkb)], axis=2)
    vb = jnp.concatenate([vp[:, j:j + nc] for j in range(nkb)], axis=2)
    key_chunk = jnp.arange(nc)[:, None] - N_BACK_CHUNKS + jnp.arange(nkb * CHUNK)[None, :] // CHUNK
    valid = key_chunk >= 0
    s = jnp.einsum('bnqkgd,bnskd->bnkgqs', qc.astype(jnp.float32), kb.astype(jnp.float32)) * (HEAD_DIM ** -0.5)
    s = jnp.where(valid[None, :, None, None, None, :], s, NEG)
    p = sink_softmax(s, sink.reshape(N_KV_HEADS, GROUP)[:, :, None, None].astype(jnp.float32))
    o = jnp.einsum('bnkgqs,bnskd->bnqkgd', p.astype(vb.dtype), vb)
    return o.reshape(B, S, D_ATTN)


def attn_sample(q, k, v, sink, k_cache, v_cache):
    B, T = q.shape[:2]
    kk = jnp.concatenate([k_cache.astype(k.dtype), k], axis=1)
    vv = jnp.concatenate([v_cache.astype(v.dtype), v], axis=1)
    qg = q.reshape(B, T, N_KV_HEADS, GROUP, HEAD_DIM)
    s = jnp.einsum('btkgd,bskd->bkgts', qg.astype(jnp.float32), kk.astype(jnp.float32)) * (HEAD_DIM ** -0.5)
    p = sink_softmax(s, sink.reshape(N_KV_HEADS, GROUP)[:, :, None, None].astype(jnp.float32))
    o = jnp.einsum('bkgts,bskd->btkgd', p.astype(vv.dtype), vv)
    return o.reshape(B, T, D_ATTN)


def layer(x, lp, lru_conv_prev, lru_h_prev, ffn_conv_prev, attn_fn):
    (g_mix, w_in, b_gate, w_lconv, b_lconv, w_a, b_a, w_i, b_i, lam, g_q, g_k, sink,
     w_br_lru, w_br_attn, w_out, g_ffn, w_up, w_fconv, b_fconv, w_down) = lp
    B, T, _ = x.shape
    xn = rmsnorm(x, g_mix)
    u_lru, y_lru, q, k, v, gates = jnp.split(xn @ w_in, IN_SPLITS, axis=-1)
    xc, lru_conv_new = causal_dwconv(u_lru, lru_conv_prev, w_lconv, b_lconv)
    h, lru_h_new = rg_lru(xc, lru_h_prev, w_a, b_a, w_i, b_i, lam)
    br_lru = (h * jax.nn.gelu(y_lru)) @ w_br_lru
    q = rmsnorm(q.reshape(B, T, N_HEADS, HEAD_DIM), g_q)
    k = rmsnorm(k.reshape(B, T, N_KV_HEADS, HEAD_DIM), g_k)
    v = v.reshape(B, T, N_KV_HEADS, HEAD_DIM)
    br_attn = attn_fn(q, k, v, sink) @ w_br_attn
    g = jax.nn.sigmoid((gates + b_gate).astype(jnp.float32)).astype(x.dtype).reshape(B, T, N_BRANCH, D_MODEL)
    x = x + (g[:, :, 0] * br_lru + g[:, :, 1] * br_attn) @ w_out
    gate_pre, up = jnp.split(rmsnorm(x, g_ffn) @ w_up, [D_FF], axis=-1)
    gc, ffn_conv_new = causal_dwconv(gate_pre, ffn_conv_prev, w_fconv, b_fconv)
    x = x + (jax.nn.gelu(gc) * up) @ w_down
    return x, k, v, lru_conv_new, lru_h_new, ffn_conv_new


def setup_inputs(seed: int = 0) -> dict:
    key = jax.random.key(seed)
    ks = jax.random.split(key, 32)
    nrm = lambda k, shape, scale: jax.random.normal(k, shape, jnp.float32) * scale
    a8 = jax.random.uniform(ks[13], (DEPTH, D_LRU), jnp.float32, 0.9, 0.999)
    base = a8 ** (1.0 / LRU_C)
    return {
        'x_prompt': nrm(ks[0], (BATCH, SEQ, D_MODEL), 1.0),
        'x_sample': nrm(ks[1], (DEC_BATCH, DEC_SEQ, D_MODEL), 1.0),
        'state_lru_conv': nrm(ks[2], (DEPTH, DEC_BATCH, LRU_CONV_W - 1, D_LRU), 1.0),
        'state_lru_h': nrm(ks[3], (DEPTH, DEC_BATCH, D_LRU), 0.5),
        'cache_k': nrm(ks[4], (DEPTH, DEC_BATCH, min(WINDOW, PAST_LEN), N_KV_HEADS, HEAD_DIM), 1.0),
        'cache_v': nrm(ks[5], (DEPTH, DEC_BATCH, min(WINDOW, PAST_LEN), N_KV_HEADS, HEAD_DIM), 1.0),
        'state_ffn_conv': nrm(ks[6], (DEPTH, DEC_BATCH, FFN_CONV_W - 1, D_FF), 1.0),
        'g_mix': 1.0 + nrm(ks[7], (DEPTH, D_MODEL), 0.02),
        'w_in': nrm(ks[8], (DEPTH, D_MODEL, D_IN), D_MODEL ** -0.5),
        'b_gate': nrm(ks[9], (DEPTH, N_BRANCH * D_MODEL), 0.02),
        'w_lru_conv': nrm(ks[10], (DEPTH, LRU_CONV_W, D_LRU), LRU_CONV_W ** -0.5),
        'b_lru_conv': nrm(ks[11], (DEPTH, D_LRU), 0.02),
        'w_lru_a': nrm(ks[12], (DEPTH, N_LRU_BLOCKS, LRU_BLOCK, LRU_BLOCK), LRU_BLOCK ** -0.5),
        'b_lru_a': nrm(ks[14], (DEPTH, D_LRU), 0.02),
        'w_lru_i': nrm(ks[15], (DEPTH, N_LRU_BLOCKS, LRU_BLOCK, LRU_BLOCK), LRU_BLOCK ** -0.5),
        'b_lru_i': nrm(ks[16], (DEPTH, D_LRU), 0.02),
        'lru_lambda': jnp.log(base) - jnp.log1p(-base),
        'g_q': 1.0 + nrm(ks[17], (DEPTH, HEAD_DIM), 0.02),
        'g_k': 1.0 + nrm(ks[18], (DEPTH, HEAD_DIM), 0.02),
        'attn_sink': nrm(ks[19], (DEPTH, N_HEADS), 0.5),
        'w_br_lru': nrm(ks[20], (DEPTH, D_LRU, D_MODEL), D_LRU ** -0.5),
        'w_br_attn': nrm(ks[21], (DEPTH, D_ATTN, D_MODEL), D_ATTN ** -0.5),
        'w_out': nrm(ks[22], (DEPTH, D_MODEL, D_MODEL), D_MODEL ** -0.5),
        'g_ffn': 1.0 + nrm(ks[23], (DEPTH, D_MODEL), 0.02),
        'w_up': nrm(ks[24], (DEPTH, D_MODEL, 2 * D_FF), D_MODEL ** -0.5),
        'w_ffn_conv': nrm(ks[25], (DEPTH, FFN_CONV_W, D_FF), FFN_CONV_W ** -0.5),
        'b_ffn_conv': nrm(ks[26], (DEPTH, D_FF), 0.02),
        'w_down': nrm(ks[27], (DEPTH, D_FF, D_MODEL), D_FF ** -0.5),
    }


def reference(x_prompt, x_sample, state_lru_conv, state_lru_h, cache_k, cache_v, state_ffn_conv,
              g_mix, w_in, b_gate, w_lru_conv, b_lru_conv, w_lru_a, b_lru_a, w_lru_i, b_lru_i,
              lru_lambda, g_q, g_k, attn_sink, w_br_lru, w_br_attn, w_out, g_ffn, w_up,
              w_ffn_conv, b_ffn_conv, w_down):
    params = (g_mix, w_in, b_gate, w_lru_conv, b_lru_conv, w_lru_a, b_lru_a, w_lru_i, b_lru_i,
              lru_lambda, g_q, g_k, attn_sink, w_br_lru, w_br_attn, w_out, g_ffn, w_up,
              w_ffn_conv, b_ffn_conv, w_down)
    xp, xs = x_prompt, x_sample
    B = xp.shape[0]
    p_lc, p_lh, p_k, p_v, p_fc = [], [], [], [], []
    s_lc, s_lh, s_k, s_v, s_fc = [], [], [], [], []
    for l in range(DEPTH):
        lp = tuple(p[l] for p in params)
        xp, kp, vp, lc, lh, fc = layer(
            xp, lp,
            jnp.zeros((B, LRU_CONV_W - 1, D_LRU), xp.dtype),
            jnp.zeros((B, D_LRU), xp.dtype),
            jnp.zeros((B, FFN_CONV_W - 1, D_FF), xp.dtype),
            attn_prompt)
        p_lc.append(lc); p_lh.append(lh); p_fc.append(fc)
        p_k.append(kp[:, kp.shape[1] - WINDOW:]); p_v.append(vp[:, vp.shape[1] - WINDOW:])
        ck, cv = cache_k[l], cache_v[l]
        xs, ks_, vs_, lc, lh, fc = layer(
            xs, lp, state_lru_conv[l], state_lru_h[l], state_ffn_conv[l],
            lambda q, k, v, s: attn_sample(q, k, v, s, ck, cv))
        s_lc.append(lc); s_lh.append(lh); s_fc.append(fc); s_k.append(ks_); s_v.append(vs_)
    return (xp, xs,
            jnp.stack(p_lc), jnp.stack(p_lh), jnp.stack(p_k), jnp.stack(p_v), jnp.stack(p_fc),
            jnp.stack(s_lc), jnp.stack(s_lh), jnp.stack(s_k), jnp.stack(s_v), jnp.stack(s_fc))
```

```python
import functools

import jax
import jax.numpy as jnp
from jax import lax
from jax.experimental import pallas as pl
from jax.experimental.pallas import tpu as pltpu

D_MODEL = 1024
N_HEADS = 16
N_KV_HEADS = 4
HEAD_DIM = 64
GROUP = N_HEADS // N_KV_HEADS
D_KV = N_KV_HEADS * HEAD_DIM
WINDOW = 128
D_LRU = D_MODEL
N_LRU_BLOCKS = 16
LRU_BLOCK = D_LRU // N_LRU_BLOCKS
LRU_C = 8.0
LRU_CONV_W = 4
D_FF = 3 * D_MODEL
FFN_CONV_W = 3
EPS = 1e-6
NEG = -1e30

MXU_TILE = 256
SUBLANES = 8
LANES = 128
N_GATE_GROUPS = D_LRU // MXU_TILE

_U0, _Y0, _Q0, _G0, _K0, _V0, _D_IN = 0, 1024, 2048, 3072, 5120, 5376, 5632

VMEM_LIMIT = 56 * 1024 * 1024

BF16 = jnp.bfloat16
F32 = jnp.float32


def _dot(a, b):
    return jnp.dot(a, b, preferred_element_type=F32)


def _rms_scale(x):
    return lax.rsqrt(jnp.mean(x * x, axis=-1, keepdims=True) + EPS)


def _const_spec(shape):
    nd = len(shape)
    return pl.BlockSpec(shape, lambda *_: (0,) * nd, pipeline_mode=pl.Buffered(1))


def _inproj_kernel(x_ref, gmix_ref, w_ref, bg_ref, gq_ref, gk_ref, gn_ref,
                   u_ref, gy_ref, q_ref, g_ref, k_ref, v_ref):
    x = x_ref[...]
    xn = (x * _rms_scale(x) * gmix_ref[...]).astype(BF16)

    def proj(lo, hi):
        return _dot(xn, w_ref[:, lo:hi])

    def head_norm(t, gain):
        ms = _dot((t * t).astype(BF16), gn_ref[...])
        return t * lax.rsqrt(ms + EPS) * gain

    u_ref[...] = proj(_U0, _Y0)
    gy_ref[...] = jax.nn.gelu(proj(_Y0, _Q0)).astype(BF16)
    scale = HEAD_DIM ** -0.5
    for j in range(D_MODEL // MXU_TILE):
        lo = _Q0 + j * MXU_TILE
        q_ref[:, j * MXU_TILE:(j + 1) * MXU_TILE] = (
            head_norm(proj(lo, lo + MXU_TILE), gq_ref[...]) * scale).astype(BF16)
    g_ref[...] = jax.nn.sigmoid(proj(_G0, _K0) + bg_ref[...]).astype(BF16)
    k_ref[...] = head_norm(proj(_K0, _V0), gk_ref[...])
    v_ref[...] = proj(_V0, _D_IN)


def _inproj(x, lw, tm):
    m = x.shape[0]
    row = lambda w: pl.BlockSpec((tm, w), lambda i: (i, 0))
    return pl.pallas_call(
        _inproj_kernel,
        grid=(m // tm,),
        in_specs=[row(D_MODEL), _const_spec((1, D_MODEL)), _const_spec((D_MODEL, _D_IN)),
                  _const_spec((1, 2 * D_MODEL)), _const_spec((1, MXU_TILE)), _const_spec((1, MXU_TILE)),
                  _const_spec((MXU_TILE, MXU_TILE))],
        out_specs=[row(D_LRU), row(D_LRU), row(D_MODEL), row(2 * D_MODEL), row(D_KV), row(D_KV)],
        out_shape=[jax.ShapeDtypeStruct((m, D_LRU), F32), jax.ShapeDtypeStruct((m, D_LRU), BF16),
                   jax.ShapeDtypeStruct((m, D_MODEL), BF16), jax.ShapeDtypeStruct((m, 2 * D_MODEL), BF16),
                   jax.ShapeDtypeStruct((m, D_KV), F32), jax.ShapeDtypeStruct((m, D_KV), F32)],
        compiler_params=pltpu.CompilerParams(vmem_limit_bytes=VMEM_LIMIT),
        name="inproj",
    )(x, lw["g_mix"], lw["w_in"], lw["b_gate"], lw["g_q"], lw["g_k"], lw["gn"])


def _lru_kernel(u_ref, gy_ref, g0_ref, cs_ref, h0_ref, wc_ref, bc_ref, wai_ref, ba_ref, bi_ref,
                lam_ref, wbr_ref, z_ref, cso_ref, ho_ref,
                ubuf, a_s, b_s, hs_s, as_s, hg_s, carry, *, tm, nseg):
    t = pl.program_id(1)
    pad = SUBLANES
    seg = tm // nseg

    @pl.when(t == 0)
    def _():
        ubuf[0:pad, :] = cs_ref[...]
        carry[...] = h0_ref[...]

    u = u_ref[...]
    ubuf[pad:pad + tm, :] = u
    xc = bc_ref[...]
    for j in range(LRU_CONV_W):
        lo = pad - (LRU_CONV_W - 1) + j
        xc = xc + wc_ref[j:j + 1, :] * ubuf[lo:lo + tm, :]
    ubuf[0:pad, :] = ubuf[tm:tm + pad, :]

    lam = lam_ref[...]
    softplus_neg_lam = jnp.maximum(-lam, 0.0) + jnp.log1p(jnp.exp(-jnp.abs(lam)))
    xcb = xc.astype(BF16)
    for j in range(N_GATE_GROUPS):
        cols = slice(j * MXU_TILE, (j + 1) * MXU_TILE)
        ri = _dot(xcb[:, cols], wai_ref[j])
        r = jax.nn.sigmoid(ri[:, :MXU_TILE] + ba_ref[:, cols])
        i = jax.nn.sigmoid(ri[:, MXU_TILE:] + bi_ref[:, cols])
        log_a = -LRU_C * r * softplus_neg_lam[:, cols]
        a = jnp.exp(log_a)
        mult = jnp.sqrt(1.0 - a * a)
        b = mult * i * xc[:, cols]
        for k in range(MXU_TILE // LANES):
            lb = j * (MXU_TILE // LANES) + k
            a_s[lb] = a[:, k * LANES:(k + 1) * LANES]
            b_s[lb] = b[:, k * LANES:(k + 1) * LANES]

    nlb = D_LRU // LANES
    hs = [jnp.zeros((nseg, LANES), F32)] * nlb
    ap = [jnp.ones((nseg, LANES), F32)] * nlb
    for j in range(seg):
        rows = pl.ds(j, nseg, stride=seg) if nseg > 1 else pl.ds(j, 1)
        for lb in range(nlb):
            a = a_s[lb, rows, :]
            hs[lb] = a * hs[lb] + b_s[lb, rows, :]
            ap[lb] = a * ap[lb]
            hs_s[lb, rows, :] = hs[lb]
            as_s[lb, rows, :] = ap[lb]
    c = carry[...]
    c = [c[:, lb * LANES:(lb + 1) * LANES] for lb in range(nlb)]
    for s in range(nseg):
        rows = slice(s * seg, (s + 1) * seg)
        for lb in range(nlb):
            cols = slice(lb * LANES, (lb + 1) * LANES)
            h = hs_s[lb, rows, :] + as_s[lb, rows, :] * c[lb]
            hg_s[rows, cols] = (h * gy_ref[rows, cols].astype(F32)).astype(BF16)
            c[lb] = ap[lb][s:s + 1, :] * c[lb] + hs[lb][s:s + 1, :]
    c = jnp.concatenate(c, axis=1)
    carry[...] = c

    z_ref[...] = g0_ref[...].astype(F32) * _dot(hg_s[...], wbr_ref[...])

    @pl.when(t == pl.num_programs(1) - 1)
    def _():
        cso_ref[...] = ubuf[pad - (LRU_CONV_W - 1):pad, :]
        ho_ref[...] = c


def _lru(u, gy, g, conv_state, h_state, lw, b, t_len, tm, nseg):
    nt = t_len // tm
    row = lambda blk: pl.BlockSpec((tm, D_LRU), lambda bi, ti: (bi * nt + ti, blk))
    per_b = lambda r: pl.BlockSpec((None, r, D_LRU), lambda bi, ti: (bi, 0, 0))
    vec = _const_spec((1, D_LRU))
    kern = functools.partial(_lru_kernel, tm=tm, nseg=nseg)
    return pl.pallas_call(
        kern,
        grid=(b, nt),
        in_specs=[row(0), row(0), row(0), per_b(SUBLANES), per_b(1),
                  _const_spec((LRU_CONV_W, D_LRU)), vec,
                  _const_spec((N_GATE_GROUPS, MXU_TILE, 2 * MXU_TILE)), vec, vec, vec,
                  _const_spec((D_LRU, D_MODEL))],
        out_specs=[row(0), per_b(LRU_CONV_W - 1), per_b(1)],
        out_shape=[jax.ShapeDtypeStruct((b * t_len, D_MODEL), F32),
                   jax.ShapeDtypeStruct((b, LRU_CONV_W - 1, D_LRU), F32),
                   jax.ShapeDtypeStruct((b, 1, D_LRU), F32)],
        scratch_shapes=[pltpu.VMEM((tm + SUBLANES, D_LRU), F32)]
                       + [pltpu.VMEM((D_LRU // LANES, tm, LANES), F32)] * 4
                       + [pltpu.VMEM((tm, D_LRU), BF16), pltpu.VMEM((1, D_LRU), F32)],
        compiler_params=pltpu.CompilerParams(vmem_limit_bytes=VMEM_LIMIT),
        name="lru",
    )(u, gy, g, conv_state, h_state, lw["w_lconv"], lw["b_lconv"], lw["w_ai"], lw["b_a"], lw["b_i"],
      lw["lam"], lw["w_br_lru"])


def _attn_kernel(q_ref, kh_ref, vh_ref, kt_ref, vt_ref, sink_ref, z_ref, g1_ref, x_ref,
                 wba_ref, wout_ref, gffn_ref, h_ref, hn_ref,
                 kbuf, vbuf, attn_s, *, tq, chunk, hist_always_valid):
    ti = pl.program_id(1)
    nch = tq // chunk
    win = WINDOW + chunk
    col_head = lax.broadcasted_iota(jnp.int32, (1, D_KV), 1) // HEAD_DIM
    for kh in range(N_KV_HEADS):
        m = (col_head == kh).astype(F32)
        kbuf[kh, 0:WINDOW, :] = (kh_ref[...] * m).astype(BF16)
        kbuf[kh, WINDOW:, :] = (kt_ref[...] * m).astype(BF16)
        vbuf[kh, 0:WINDOW, :] = (vh_ref[...] * m).astype(BF16)
        vbuf[kh, WINDOW:, :] = (vt_ref[...] * m).astype(BF16)

    for c in range(nch):
        r0 = c * chunk
        qst = jnp.concatenate(
            [q_ref[r0:r0 + chunk, g * D_KV:(g + 1) * D_KV] for g in range(GROUP)], axis=0)
        masked = (not hist_always_valid) and r0 < WINDOW
        if masked:
            key_row = r0 + lax.broadcasted_iota(jnp.int32, (1, win), 1)
            valid = jnp.logical_or(key_row >= WINDOW, ti > 0)
        o = jnp.zeros((GROUP * chunk, D_KV), F32)
        for kh in range(N_KV_HEADS):
            s = lax.dot_general(qst, kbuf[kh, r0:r0 + win, :], (((1,), (1,)), ((), ())),
                                preferred_element_type=F32)
            if masked:
                s = jnp.where(valid, s, NEG)
            sink = sink_ref[kh]
            mx = jnp.maximum(jnp.max(s, axis=-1, keepdims=True), sink)
            e = jnp.exp(s - mx)
            den = jnp.sum(e, axis=-1, keepdims=True) + jnp.exp(sink - mx)
            p = (e * (1.0 / den)).astype(BF16)
            o = o + _dot(p, vbuf[kh, r0:r0 + win, :])
        for g in range(GROUP):
            attn_s[r0:r0 + chunk, g * D_KV:(g + 1) * D_KV] = o[g * chunk:(g + 1) * chunk, :].astype(BF16)

    br = _dot(attn_s[...], wba_ref[...])
    mix = (z_ref[...] + g1_ref[...].astype(F32) * br).astype(BF16)
    h = x_ref[...] + _dot(mix, wout_ref[...])
    h_ref[...] = h
    hn_ref[...] = (h * _rms_scale(h) * gffn_ref[...]).astype(BF16)


def _attn(q, k, v, k_hist, v_hist, z, g, x, lw, b, t_len, tq, chunk, hist_is_cache):
    nt = t_len // tq
    row = lambda w, blk=0: pl.BlockSpec((tq, w), lambda bi, ti: (bi * nt + ti, blk))
    if hist_is_cache:
        hist = pl.BlockSpec((WINDOW, D_KV), lambda bi, ti: (bi, 0))
    else:
        per_b, per_t = t_len // WINDOW, tq // WINDOW
        hist = pl.BlockSpec((WINDOW, D_KV), lambda bi, ti: (jnp.maximum(bi * per_b + ti * per_t - 1, 0), 0))
    kern = functools.partial(_attn_kernel, tq=tq, chunk=chunk, hist_always_valid=hist_is_cache)
    return pl.pallas_call(
        kern,
        grid=(b, nt),
        in_specs=[row(D_MODEL), hist, hist, row(D_KV), row(D_KV),
                  _const_spec((N_KV_HEADS, GROUP * chunk, 1)),
                  row(D_MODEL), row(D_MODEL, 1), row(D_MODEL),
                  _const_spec((D_MODEL, D_MODEL)), _const_spec((D_MODEL, D_MODEL)), _const_spec((1, D_MODEL))],
        out_specs=[row(D_MODEL), row(D_MODEL)],
        out_shape=[jax.ShapeDtypeStruct((b * t_len, D_MODEL), F32),
                   jax.ShapeDtypeStruct((b * t_len, D_MODEL), BF16)],
        scratch_shapes=[pltpu.VMEM((N_KV_HEADS, WINDOW + tq, D_KV), BF16)] * 2
                       + [pltpu.VMEM((tq, D_MODEL), BF16)],
        compiler_params=pltpu.CompilerParams(vmem_limit_bytes=VMEM_LIMIT),
        name="attn",
    )(q, k_hist, v_hist, k, v, lw["sink_rows"][chunk], z, g, x, lw["w_br_attn"], lw["w_out"], lw["g_ffn"])


def _ffn_kernel(hn_ref, h_ref, fs_ref, wup_ref, wfc_ref, bfc_ref, wdn_ref, y_ref, fso_ref, gbuf, *, tm):
    t = pl.program_id(1)
    pad = SUBLANES

    @pl.when(t == 0)
    def _():
        gbuf[0:pad, :] = fs_ref[...]

    hn = hn_ref[...]
    gate = _dot(hn, wup_ref[:, :D_FF])
    up = _dot(hn, wup_ref[:, D_FF:])
    gbuf[pad:pad + tm, :] = gate
    gc = bfc_ref[...]
    for j in range(FFN_CONV_W):
        lo = pad - (FFN_CONV_W - 1) + j
        gc = gc + wfc_ref[j:j + 1, :] * gbuf[lo:lo + tm, :]
    gbuf[0:pad, :] = gbuf[tm:tm + pad, :]
    act = (jax.nn.gelu(gc) * up).astype(BF16)
    y_ref[...] = h_ref[...] + _dot(act, wdn_ref[...])

    @pl.when(t == pl.num_programs(1) - 1)
    def _():
        fso_ref[...] = gbuf[pad - (FFN_CONV_W - 1):pad, :]


def _ffn(hn, h, ffn_state, lw, b, t_len, tm):
    nt = t_len // tm
    row = pl.BlockSpec((tm, D_MODEL), lambda bi, ti: (bi * nt + ti, 0))
    per_b = lambda r: pl.BlockSpec((None, r, D_FF), lambda bi, ti: (bi, 0, 0))
    kern = functools.partial(_ffn_kernel, tm=tm)
    return pl.pallas_call(
        kern,
        grid=(b, nt),
        in_specs=[row, row, per_b(SUBLANES), _const_spec((D_MODEL, 2 * D_FF)),
                  _const_spec((FFN_CONV_W, D_FF)), _const_spec((1, D_FF)), _const_spec((D_FF, D_MODEL))],
        out_specs=[row, per_b(FFN_CONV_W - 1)],
        out_shape=[jax.ShapeDtypeStruct((b * t_len, D_MODEL), F32),
                   jax.ShapeDtypeStruct((b, FFN_CONV_W - 1, D_FF), F32)],
        scratch_shapes=[pltpu.VMEM((tm + SUBLANES, D_FF), F32)],
        compiler_params=pltpu.CompilerParams(vmem_limit_bytes=VMEM_LIMIT),
        name="ffn",
    )(hn, h, ffn_state, lw["w_up"], lw["w_fconv"], lw["b_fconv"], lw["w_down"])


def _block_diag_groups(w):
    per = MXU_TILE // LRU_BLOCK
    w4 = w.reshape(N_GATE_GROUPS, per, LRU_BLOCK, LRU_BLOCK)
    return jnp.einsum("jncd,nm->jncmd", w4, jnp.eye(per, dtype=w.dtype)).reshape(
        N_GATE_GROUPS, MXU_TILE, MXU_TILE)


def _prep_layer(l, p, chunks):
    w_in = p["w_in"][l]
    wq = w_in[:, 2048:3072].reshape(D_MODEL, N_KV_HEADS, GROUP, HEAD_DIM).transpose(0, 2, 1, 3)
    w_in_p = jnp.concatenate(
        [w_in[:, :2048], wq.reshape(D_MODEL, D_MODEL), w_in[:, 3584:], w_in[:, 3072:3584]], axis=1)
    w_ba = p["w_br_attn"][l].reshape(N_KV_HEADS, GROUP, HEAD_DIM, D_MODEL).transpose(1, 0, 2, 3)
    per = MXU_TILE // HEAD_DIM
    sink = p["attn_sink"][l].reshape(N_KV_HEADS, GROUP)
    row = lambda v: v.reshape(1, -1)
    return {
        "g_mix": row(p["g_mix"][l]),
        "w_in": w_in_p.astype(BF16),
        "b_gate": row(p["b_gate"][l]),
        "g_q": row(jnp.tile(p["g_q"][l], per)),
        "g_k": row(jnp.tile(p["g_k"][l], per)),
        "gn": (jnp.kron(jnp.eye(per, dtype=F32), jnp.ones((HEAD_DIM, HEAD_DIM), F32)) / HEAD_DIM).astype(BF16),
        "w_lconv": p["w_lru_conv"][l],
        "b_lconv": row(p["b_lru_conv"][l]),
        "w_ai": jnp.concatenate([_block_diag_groups(p["w_lru_a"][l]), _block_diag_groups(p["w_lru_i"][l])],
                                axis=2).astype(BF16),
        "b_a": row(p["b_lru_a"][l]),
        "b_i": row(p["b_lru_i"][l]),
        "lam": row(p["lru_lambda"][l]),
        "w_br_lru": p["w_br_lru"][l].astype(BF16),
        "sink_rows": {c: jnp.repeat(sink, c, axis=1)[:, :, None] for c in chunks},
        "w_br_attn": w_ba.reshape(D_MODEL, D_MODEL).astype(BF16),
        "w_out": p["w_out"][l].astype(BF16),
        "g_ffn": row(p["g_ffn"][l]),
        "w_up": p["w_up"][l].astype(BF16),
        "w_fconv": p["w_ffn_conv"][l],
        "b_fconv": row(p["b_ffn_conv"][l]),
        "w_down": p["w_down"][l].astype(BF16),
    }


def _pick(n, pref):
    return pref if n % pref == 0 else n


def _layer(x, lw, conv_state, h_state, ffn_state, k_hist, v_hist, b, t_len, chunk, hist_is_cache):
    m = b * t_len
    u, gy, q, g, k, v = _inproj(x, lw, _pick(m, 512))
    tm = _pick(t_len, 256)
    nseg = SUBLANES if (tm // SUBLANES) % 16 == 0 else 1
    z, conv_new, h_new = _lru(u, gy, g, conv_state, h_state, lw, b, t_len, tm, nseg)
    if not hist_is_cache:
        k_hist, v_hist = k, v
    h, hn = _attn(q, k, v, k_hist, v_hist, z, g, x, lw, b, t_len, _pick(t_len, 256), chunk, hist_is_cache)
    y, ffn_new = _ffn(hn, h, ffn_state, lw, b, t_len, _pick(t_len, 256))
    return y, k, v, conv_new, h_new, ffn_new


def _front_pad(state):
    return jnp.pad(state, ((0, 0), (SUBLANES - state.shape[1], 0), (0, 0)))


def kernel(x_prompt, x_sample, state_lru_conv, state_lru_h, cache_k, cache_v, state_ffn_conv, g_mix, w_in, b_gate, w_lru_conv, b_lru_conv, w_lru_a, b_lru_a, w_lru_i, b_lru_i, lru_lambda, g_q, g_k, attn_sink, w_br_lru, w_br_attn, w_out, g_ffn, w_up, w_ffn_conv, b_ffn_conv, w_down):
    params = dict(g_mix=g_mix, w_in=w_in, b_gate=b_gate, w_lru_conv=w_lru_conv, b_lru_conv=b_lru_conv,
                  w_lru_a=w_lru_a, b_lru_a=b_lru_a, w_lru_i=w_lru_i, b_lru_i=b_lru_i, lru_lambda=lru_lambda,
                  g_q=g_q, g_k=g_k, attn_sink=attn_sink, w_br_lru=w_br_lru, w_br_attn=w_br_attn, w_out=w_out,
                  g_ffn=g_ffn, w_up=w_up, w_ffn_conv=w_ffn_conv, b_ffn_conv=b_ffn_conv, w_down=w_down)
    depth = w_in.shape[0]
    bp, sp, _ = x_prompt.shape
    bs, ss, _ = x_sample.shape
    p_chunk, s_chunk = 64, ss
    xp = x_prompt.reshape(bp * sp, D_MODEL)
    xs = x_sample.reshape(bs * ss, D_MODEL)
    zeros_p = (jnp.zeros((bp, SUBLANES, D_LRU), F32), jnp.zeros((bp, 1, D_LRU), F32),
               jnp.zeros((bp, SUBLANES, D_FF), F32))
    outs = {n: [] for n in ("p_lc", "p_lh", "p_k", "p_v", "p_fc", "s_lc", "s_lh", "s_k", "s_v", "s_fc")}
    for l in range(depth):
        lw = _prep_layer(l, params, {p_chunk, s_chunk})
        xp, k, v, lc, lh, fc = _layer(xp, lw, *zeros_p, None, None, bp, sp, p_chunk, False)
        outs["p_lc"].append(lc)
        outs["p_lh"].append(lh.reshape(bp, D_LRU))
        outs["p_k"].append(k.reshape(bp, sp, N_KV_HEADS, HEAD_DIM)[:, sp - WINDOW:])
        outs["p_v"].append(v.reshape(bp, sp, N_KV_HEADS, HEAD_DIM)[:, sp - WINDOW:])
        outs["p_fc"].append(fc)
        xs, k, v, lc, lh, fc = _layer(
            xs, lw, _front_pad(state_lru_conv[l]), state_lru_h[l][:, None, :], _front_pad(state_ffn_conv[l]),
            cache_k[l].reshape(bs * WINDOW, D_KV), cache_v[l].reshape(bs * WINDOW, D_KV),
            bs, ss, s_chunk, True)
        outs["s_lc"].append(lc)
        outs["s_lh"].append(lh.reshape(bs, D_LRU))
        outs["s_k"].append(k.reshape(bs, ss, N_KV_HEADS, HEAD_DIM))
        outs["s_v"].append(v.reshape(bs, ss, N_KV_HEADS, HEAD_DIM))
        outs["s_fc"].append(fc)
    st = {n: jnp.stack(v) for n, v in outs.items()}
    return (xp.reshape(bp, sp, D_MODEL), xs.reshape(bs, ss, D_MODEL),
            st["p_lc"], st["p_lh"], st["p_k"], st["p_v"], st["p_fc"],
            st["s_lc"], st["s_lh"], st["s_k"], st["s_v"], st["s_fc"])
```

```python
import functools

import jax
import jax.numpy as jnp
from jax import lax
from jax.experimental import pallas as pl
from jax.experimental.pallas import tpu as pltpu

D_MODEL = 1024
N_HEADS = 16
N_KV_HEADS = 4
HEAD_DIM = 64
GROUP = N_HEADS // N_KV_HEADS
D_KV = N_KV_HEADS * HEAD_DIM
WINDOW = 128
D_LRU = D_MODEL
N_LRU_BLOCKS = 16
LRU_BLOCK = D_LRU // N_LRU_BLOCKS
LRU_C = 8.0
LRU_CONV_W = 4
D_FF = 3 * D_MODEL
FFN_CONV_W = 3
EPS = 1e-6
NEG = -1e30

MXU_TILE = 256
SUBLANES = 8
LANES = 128
SEG = 32
SEG_PITCH = SEG + SUBLANES
TINY = 1e-30
N_GATE_GROUPS = D_LRU // MXU_TILE

_U0, _Y0, _Q0, _G0, _K0, _V0, _D_IN = 0, 1024, 2048, 3072, 5120, 5376, 5632

VMEM_LIMIT = 56 * 1024 * 1024

BF16 = jnp.bfloat16
F32 = jnp.float32


def _dot(a, b):
    return jnp.dot(a, b, preferred_element_type=F32)


def _rms_scale(x):
    return lax.rsqrt(jnp.mean(x * x, axis=-1, keepdims=True) + EPS)


def _const_spec(shape):
    nd = len(shape)
    return pl.BlockSpec(shape, lambda *_: (0,) * nd, pipeline_mode=pl.Buffered(1))


def _inproj_kernel(x_ref, gmix_ref, w_ref, bg_ref, gq_ref, gk_ref, gn_ref,
                   u_ref, gy_ref, q_ref, g_ref, k_ref, v_ref):
    x = x_ref[...]
    xn = (x * _rms_scale(x) * gmix_ref[...]).astype(BF16)

    def proj(lo, hi):
        return _dot(xn, w_ref[:, lo:hi])

    def head_norm(t, gain):
        ms = _dot((t * t).astype(BF16), gn_ref[...])
        return t * lax.rsqrt(ms + EPS) * gain

    u_ref[...] = proj(_U0, _Y0)
    gy_ref[...] = jax.nn.gelu(proj(_Y0, _Q0)).astype(BF16)
    scale = HEAD_DIM ** -0.5
    for j in range(D_MODEL // MXU_TILE):
        lo = _Q0 + j * MXU_TILE
        q_ref[:, j * MXU_TILE:(j + 1) * MXU_TILE] = (
            head_norm(proj(lo, lo + MXU_TILE), gq_ref[...]) * scale).astype(BF16)
    g_ref[...] = jax.nn.sigmoid(proj(_G0, _K0) + bg_ref[...]).astype(BF16)
    k_ref[...] = head_norm(proj(_K0, _V0), gk_ref[...])
    v_ref[...] = proj(_V0, _D_IN)


def _inproj(x, lw, tm):
    m = x.shape[0]
    row = lambda w: pl.BlockSpec((tm, w), lambda i: (i, 0))
    return pl.pallas_call(
        _inproj_kernel,
        grid=(m // tm,),
        in_specs=[row(D_MODEL), _const_spec((1, D_MODEL)), _const_spec((D_MODEL, _D_IN)),
                  _const_spec((1, 2 * D_MODEL)), _const_spec((1, MXU_TILE)), _const_spec((1, MXU_TILE)),
                  _const_spec((MXU_TILE, MXU_TILE))],
        out_specs=[row(D_LRU), row(D_LRU), row(D_MODEL), row(2 * D_MODEL), row(D_KV), row(D_KV)],
        out_shape=[jax.ShapeDtypeStruct((m, D_LRU), F32), jax.ShapeDtypeStruct((m, D_LRU), BF16),
                   jax.ShapeDtypeStruct((m, D_MODEL), BF16), jax.ShapeDtypeStruct((m, 2 * D_MODEL), BF16),
                   jax.ShapeDtypeStruct((m, D_KV), F32), jax.ShapeDtypeStruct((m, D_KV), F32)],
        compiler_params=pltpu.CompilerParams(vmem_limit_bytes=VMEM_LIMIT),
        name="inproj",
    )(x, lw["g_mix"], lw["w_in"], lw["b_gate"], lw["g_q"], lw["g_k"], lw["gn"])


def _lru_kernel(u_ref, gy_ref, g0_ref, cs_ref, h0_ref, wc_ref, bc_ref, wai_ref, ba_ref, bi_ref,
                lam_ref, wbr_ref, z_ref, cso_ref, ho_ref,
                us, gys, brs, carry_u, carry_h, *, nseg, chained):
    t = pl.program_id(1)
    tm = nseg * SEG
    nlb = D_LRU // LANES
    taps = LRU_CONV_W - 1

    if chained:
        @pl.when(t == 0)
        def _():
            carry_u[...] = cs_ref[...]
            carry_h[...] = h0_ref[...]

    for lb in range(nlb):
        cols = slice(lb * LANES, (lb + 1) * LANES)
        for r in range(nseg):
            base = r * SEG_PITCH
            rows = slice(r * SEG, (r + 1) * SEG)
            if not chained:
                prev = cs_ref[r, :, cols]
            elif r == 0:
                prev = carry_u[:, cols]
            else:
                prev = u_ref[r * SEG - SUBLANES:r * SEG, cols]
            us[lb, base:base + SUBLANES, :] = prev
            us[lb, base + SUBLANES:base + SEG_PITCH, :] = u_ref[rows, cols]
            gys[lb, base + SUBLANES:base + SEG_PITCH, :] = gy_ref[rows, cols].astype(F32)

    def seg_rows(i):
        return pl.ds(SUBLANES + i, nseg, stride=SEG_PITCH)

    xc_lb, u_tail = [], []
    for lb in range(nlb):
        cols = slice(lb * LANES, (lb + 1) * LANES)
        w = [jnp.broadcast_to(wc_ref[k:k + 1, cols], (nseg, LANES)) for k in range(LRU_CONV_W)]
        bias = jnp.broadcast_to(bc_ref[:, cols], (nseg, LANES))
        ut = [us[lb, seg_rows(i - taps), :] for i in range(SEG + taps)]
        steps = []
        for j in range(SEG):
            x = bias
            for k in range(LRU_CONV_W):
                x = x + w[k] * ut[j + k]
            steps.append(x)
        xc_lb.append(jnp.concatenate(steps, axis=0))
        u_tail.append(ut[SEG:])
    xc = jnp.concatenate(xc_lb, axis=1)

    lam = lam_ref[...]
    softplus_neg_lam = jnp.maximum(-lam, 0.0) + jnp.log1p(jnp.exp(-jnp.abs(lam)))
    xcb = xc.astype(BF16)
    a_lb, b_lb = [], []
    for j in range(N_GATE_GROUPS):
        cols = slice(j * MXU_TILE, (j + 1) * MXU_TILE)
        ri = _dot(xcb[:, cols], wai_ref[j])
        r = jax.nn.sigmoid(ri[:, :MXU_TILE] + ba_ref[:, cols])
        i = jax.nn.sigmoid(ri[:, MXU_TILE:] + bi_ref[:, cols])
        a = jnp.exp(-LRU_C * r * softplus_neg_lam[:, cols])
        one_minus_a2 = 1.0 - a * a
        mult = one_minus_a2 * lax.rsqrt(jnp.maximum(one_minus_a2, TINY))
        b = mult * i * xc[:, cols]
        for k in range(MXU_TILE // LANES):
            a_lb.append(a[:, k * LANES:(k + 1) * LANES])
            b_lb.append(b[:, k * LANES:(k + 1) * LANES])

    hg_lb, h_last, c_next = [], [], []
    for lb in range(nlb):
        cols = slice(lb * LANES, (lb + 1) * LANES)
        step = lambda v, j: v[j * nseg:(j + 1) * nseg, :]
        if chained:
            hs, ap = jnp.zeros((nseg, LANES), F32), jnp.ones((nseg, LANES), F32)
            hs_l, ap_l = [], []
            for j in range(SEG):
                a = step(a_lb[lb], j)
                hs = a * hs + step(b_lb[lb], j)
                ap = a * ap
                hs_l.append(hs)
                ap_l.append(ap)
            c = carry_h[:, cols]
            cin = []
            for r in range(nseg):
                cin.append(c)
                c = ap[r:r + 1, :] * c + hs[r:r + 1, :]
            c_next.append(c)
            cin = jnp.concatenate(cin, axis=0)
            h_l = [hs_l[j] + ap_l[j] * cin for j in range(SEG)]
        else:
            h, h_l = h0_ref[:, cols], []
            for j in range(SEG):
                h = step(a_lb[lb], j) * h + step(b_lb[lb], j)
                h_l.append(h)
            h_last.append(h)
        hg_lb.append(jnp.concatenate(
            [h_l[j] * gys[lb, seg_rows(j), :] for j in range(SEG)], axis=0))
    hg = jnp.concatenate(hg_lb, axis=1).astype(BF16)

    br = _dot(hg, wbr_ref[...])
    for lb in range(nlb):
        cols = slice(lb * LANES, (lb + 1) * LANES)
        for j in range(SEG):
            brs[lb, seg_rows(j), :] = br[j * nseg:(j + 1) * nseg, cols]
    for lb in range(nlb):
        cols = slice(lb * LANES, (lb + 1) * LANES)
        for r in range(nseg):
            base = r * SEG_PITCH
            rows = slice(r * SEG, (r + 1) * SEG)
            z_ref[rows, cols] = g0_ref[rows, cols].astype(F32) * brs[lb, base + SUBLANES:base + SEG_PITCH, :]

    if chained:
        carry_u[...] = u_ref[tm - SUBLANES:tm, :]
        c = jnp.concatenate(c_next, axis=1)
        carry_h[...] = c

        @pl.when(t == pl.num_programs(1) - 1)
        def _():
            cso_ref[...] = u_ref[tm - taps:tm, :]
            ho_ref[...] = c
    else:
        for k in range(taps):
            cso_ref[k] = jnp.concatenate([u_tail[lb][k] for lb in range(nlb)], axis=1)
        ho_ref[...] = jnp.concatenate(h_last, axis=1)


def _lru(u, gy, g, conv_state, h_state, lw, b, t_len, chained):
    taps = LRU_CONV_W - 1
    if chained:
        nseg = SUBLANES
        tm = nseg * SEG
        nt = t_len // tm
        grid = (b, nt)
        row = lambda blk: pl.BlockSpec((tm, D_LRU), lambda bi, ti: (bi * nt + ti, blk))
        per_b = lambda r: pl.BlockSpec((None, r, D_LRU), lambda bi, ti: (bi, 0, 0))
        state_specs = [per_b(SUBLANES), per_b(1)]
        out_state_specs = [per_b(taps), per_b(1)]
        out_state_shapes = [jax.ShapeDtypeStruct((b, taps, D_LRU), F32), jax.ShapeDtypeStruct((b, 1, D_LRU), F32)]
    else:
        assert t_len == SEG
        nseg = SUBLANES if b % SUBLANES == 0 else b
        tm = nseg * SEG
        grid = (b // nseg, 1)
        row = lambda blk: pl.BlockSpec((tm, D_LRU), lambda bi, ti: (bi, blk))
        state_specs = [pl.BlockSpec((nseg, SUBLANES, D_LRU), lambda bi, ti: (bi, 0, 0)),
                       pl.BlockSpec((nseg, D_LRU), lambda bi, ti: (bi, 0))]
        out_state_specs = [pl.BlockSpec((taps, nseg, D_LRU), lambda bi, ti: (0, bi, 0)),
                           pl.BlockSpec((nseg, D_LRU), lambda bi, ti: (bi, 0))]
        out_state_shapes = [jax.ShapeDtypeStruct((taps, b, D_LRU), F32), jax.ShapeDtypeStruct((b, D_LRU), F32)]
    vec = _const_spec((1, D_LRU))
    slab = pltpu.VMEM((D_LRU // LANES, nseg * SEG_PITCH, LANES), F32)
    kern = functools.partial(_lru_kernel, nseg=nseg, chained=chained)
    return pl.pallas_call(
        kern,
        grid=grid,
        in_specs=[row(0), row(0), row(0)] + state_specs
                 + [_const_spec((LRU_CONV_W, D_LRU)), vec,
                    _const_spec((N_GATE_GROUPS, MXU_TILE, 2 * MXU_TILE)), vec, vec, vec,
                    _const_spec((D_LRU, D_MODEL))],
        out_specs=[row(0)] + out_state_specs,
        out_shape=[jax.ShapeDtypeStruct((b * t_len, D_MODEL), F32)] + out_state_shapes,
        scratch_shapes=[slab, slab, slab, pltpu.VMEM((SUBLANES, D_LRU), F32), pltpu.VMEM((1, D_LRU), F32)],
        compiler_params=pltpu.CompilerParams(vmem_limit_bytes=VMEM_LIMIT),
        name="lru",
    )(u, gy, g, conv_state, h_state, lw["w_lconv"], lw["b_lconv"], lw["w_ai"], lw["b_a"], lw["b_i"],
      lw["lam"], lw["w_br_lru"])


def _attn_kernel(q_ref, kh_ref, vh_ref, kt_ref, vt_ref, fill_ref, z_ref, g1_ref, x_ref,
                 wba_ref, wout_ref, gffn_ref, h_ref, hn_ref,
                 kbuf, vbuf, attn_s, *, tq, chunk, chained):
    ti = pl.program_id(1)
    nunits = tq // chunk
    win = WINDOW + chunk
    pad = MXU_TILE - win
    col_head = lax.broadcasted_iota(jnp.int32, (1, D_KV), 1) // HEAD_DIM
    for kh in range(N_KV_HEADS):
        m = (col_head == kh).astype(F32)
        for src_h, src_t, buf in ((kh_ref, kt_ref, kbuf), (vh_ref, vt_ref, vbuf)):
            if chained:
                buf[kh, 0:WINDOW, :] = (src_h[...] * m).astype(BF16)
                buf[kh, WINDOW:, :] = (src_t[...] * m).astype(BF16)
            else:
                for s in range(nunits):
                    buf[kh, s * win:s * win + WINDOW, :] = (
                        src_h[s * WINDOW:(s + 1) * WINDOW, :] * m).astype(BF16)
                    buf[kh, s * win + WINDOW:(s + 1) * win, :] = (
                        src_t[s * chunk:(s + 1) * chunk, :] * m).astype(BF16)

    zero_keys = jnp.zeros((pad, D_KV), BF16)
    key_pos = lax.broadcasted_iota(jnp.int32, (1, MXU_TILE), 1)
    for c in range(nunits):
        r0 = c * chunk
        k0 = r0 if chained else c * win
        qst = jnp.concatenate(
            [q_ref[r0:r0 + chunk, g * D_KV:(g + 1) * D_KV] for g in range(GROUP)], axis=0)

        def keys(buf):
            return jnp.concatenate(
                [blk for kh in range(N_KV_HEADS) for blk in (zero_keys, buf[kh, k0:k0 + win, :])], axis=0)

        s_all = lax.dot_general(qst, keys(kbuf), (((1,), (1,)), ((), ())), preferred_element_type=F32)
        valid = key_pos >= pad
        if chained and r0 < WINDOW:
            valid = jnp.logical_and(valid, jnp.logical_or(key_pos >= pad + WINDOW - r0, ti > 0))
        probs = []
        for kh in range(N_KV_HEADS):
            s = jnp.where(valid, s_all[:, kh * MXU_TILE:(kh + 1) * MXU_TILE], fill_ref[kh])
            e = jnp.exp(s - jnp.max(s, axis=-1, keepdims=True))
            probs.append((e * (1.0 / jnp.sum(e, axis=-1, keepdims=True))).astype(BF16))
        o = _dot(jnp.concatenate(probs, axis=1), keys(vbuf))
        for g in range(GROUP):
            attn_s[r0:r0 + chunk, g * D_KV:(g + 1) * D_KV] = o[g * chunk:(g + 1) * chunk, :].astype(BF16)

    br = _dot(attn_s[...], wba_ref[...])
    mix = (z_ref[...] + g1_ref[...].astype(F32) * br).astype(BF16)
    h = x_ref[...] + _dot(mix, wout_ref[...])
    h_ref[...] = h
    hn_ref[...] = (h * _rms_scale(h) * gffn_ref[...]).astype(BF16)


def _attn(q, k, v, k_hist, v_hist, z, g, x, lw, b, t_len, chunk, chained):
    if chained:
        tq = _pick(t_len, 256)
        nt = t_len // tq
        grid = (b, nt)
        row = lambda w, blk=0: pl.BlockSpec((tq, w), lambda bi, ti: (bi * nt + ti, blk))
        per_b, per_t = t_len // WINDOW, tq // WINDOW
        hist = pl.BlockSpec((WINDOW, D_KV), lambda bi, ti: (jnp.maximum(bi * per_b + ti * per_t - 1, 0), 0))
        kv_rows = WINDOW + tq
    else:
        assert t_len == chunk
        ns = SUBLANES if b % SUBLANES == 0 else b
        tq = ns * chunk
        grid = (b // ns, 1)
        row = lambda w, blk=0: pl.BlockSpec((tq, w), lambda bi, ti: (bi, blk))
        hist = pl.BlockSpec((ns * WINDOW, D_KV), lambda bi, ti: (bi, 0))
        kv_rows = ns * (WINDOW + chunk)
    kern = functools.partial(_attn_kernel, tq=tq, chunk=chunk, chained=chained)
    return pl.pallas_call(
        kern,
        grid=grid,
        in_specs=[row(D_MODEL), hist, hist, row(D_KV), row(D_KV),
                  _const_spec((N_KV_HEADS, GROUP * chunk, MXU_TILE)),
                  row(D_MODEL), row(D_MODEL, 1), row(D_MODEL),
                  _const_spec((D_MODEL, D_MODEL)), _const_spec((D_MODEL, D_MODEL)), _const_spec((1, D_MODEL))],
        out_specs=[row(D_MODEL), row(D_MODEL)],
        out_shape=[jax.ShapeDtypeStruct((b * t_len, D_MODEL), F32),
                   jax.ShapeDtypeStruct((b * t_len, D_MODEL), BF16)],
        scratch_shapes=[pltpu.VMEM((N_KV_HEADS, kv_rows, D_KV), BF16)] * 2
                       + [pltpu.VMEM((tq, D_MODEL), BF16)],
        compiler_params=pltpu.CompilerParams(vmem_limit_bytes=VMEM_LIMIT),
        name="attn",
    )(q, k_hist, v_hist, k, v, lw["sink_fill"][chunk], z, g, x, lw["w_br_attn"], lw["w_out"], lw["g_ffn"])


def _ffn_kernel(hn_ref, h_ref, fs_ref, wup_ref, wfc_ref, bfc_ref, wdn_ref, y_ref, fso_ref, gbuf,
                *, tm, nstreams, chained):
    t = pl.program_id(1)
    taps = FFN_CONV_W - 1
    seg = tm // nstreams
    pitch = seg + SUBLANES

    if chained:
        @pl.when(t == 0)
        def _():
            gbuf[0:SUBLANES, :] = fs_ref[...]

    hn = hn_ref[...]
    gate = _dot(hn, wup_ref[:, :D_FF])
    up = _dot(hn, wup_ref[:, D_FF:])
    parts = []
    for r in range(nstreams):
        base = r * pitch
        if not chained:
            gbuf[base:base + SUBLANES, :] = fs_ref[r]
        gbuf[base + SUBLANES:base + pitch, :] = gate[r * seg:(r + 1) * seg, :]
        gc = bfc_ref[...]
        for j in range(FFN_CONV_W):
            lo = base + SUBLANES - taps + j
            gc = gc + wfc_ref[j:j + 1, :] * gbuf[lo:lo + seg, :]
        parts.append(gc)
    gc = parts[0] if nstreams == 1 else jnp.concatenate(parts, axis=0)
    act = (jax.nn.gelu(gc) * up).astype(BF16)
    y_ref[...] = h_ref[...] + _dot(act, wdn_ref[...])

    if chained:
        gbuf[0:SUBLANES, :] = gbuf[tm:tm + SUBLANES, :]

        @pl.when(t == pl.num_programs(1) - 1)
        def _():
            fso_ref[...] = gbuf[SUBLANES - taps:SUBLANES, :]
    else:
        for r in range(nstreams):
            fso_ref[r] = gbuf[(r + 1) * pitch - taps:(r + 1) * pitch, :]


def _ffn(hn, h, ffn_state, lw, b, t_len, chained):
    taps = FFN_CONV_W - 1
    if chained:
        tm, ns = _pick(t_len, 256), 1
        nt = t_len // tm
        grid = (b, nt)
        row = pl.BlockSpec((tm, D_MODEL), lambda bi, ti: (bi * nt + ti, 0))
        state = lambda r: pl.BlockSpec((None, r, D_FF), lambda bi, ti: (bi, 0, 0))
    else:
        ns = SUBLANES if b % SUBLANES == 0 else b
        tm = ns * t_len
        grid = (b // ns, 1)
        row = pl.BlockSpec((tm, D_MODEL), lambda bi, ti: (bi, 0))
        state = lambda r: pl.BlockSpec((ns, r, D_FF), lambda bi, ti: (bi, 0, 0))
    kern = functools.partial(_ffn_kernel, tm=tm, nstreams=ns, chained=chained)
    return pl.pallas_call(
        kern,
        grid=grid,
        in_specs=[row, row, state(SUBLANES), _const_spec((D_MODEL, 2 * D_FF)),
                  _const_spec((FFN_CONV_W, D_FF)), _const_spec((1, D_FF)), _const_spec((D_FF, D_MODEL))],
        out_specs=[row, state(taps)],
        out_shape=[jax.ShapeDtypeStruct((b * t_len, D_MODEL), F32),
                   jax.ShapeDtypeStruct((b, taps, D_FF), F32)],
        scratch_shapes=[pltpu.VMEM((ns * (tm // ns + SUBLANES), D_FF), F32)],
        compiler_params=pltpu.CompilerParams(vmem_limit_bytes=VMEM_LIMIT),
        name="ffn",
    )(hn, h, ffn_state, lw["w_up"], lw["w_fconv"], lw["b_fconv"], lw["w_down"])


def _block_diag_groups(w):
    per = MXU_TILE // LRU_BLOCK
    w4 = w.reshape(N_GATE_GROUPS, per, LRU_BLOCK, LRU_BLOCK)
    return jnp.einsum("jncd,nm->jncmd", w4, jnp.eye(per, dtype=w.dtype)).reshape(
        N_GATE_GROUPS, MXU_TILE, MXU_TILE)


def _prep_layer(l, p, chunks):
    w_in = p["w_in"][l]
    wq = w_in[:, 2048:3072].reshape(D_MODEL, N_KV_HEADS, GROUP, HEAD_DIM).transpose(0, 2, 1, 3)
    w_in_p = jnp.concatenate(
        [w_in[:, :2048], wq.reshape(D_MODEL, D_MODEL), w_in[:, 3584:], w_in[:, 3072:3584]], axis=1)
    w_ba = p["w_br_attn"][l].reshape(N_KV_HEADS, GROUP, HEAD_DIM, D_MODEL).transpose(1, 0, 2, 3)
    per = MXU_TILE // HEAD_DIM
    sink = p["attn_sink"][l].reshape(N_KV_HEADS, GROUP)
    row = lambda v: v.reshape(1, -1)
    return {
        "g_mix": row(p["g_mix"][l]),
        "w_in": w_in_p.astype(BF16),
        "b_gate": row(p["b_gate"][l]),
        "g_q": row(jnp.tile(p["g_q"][l], per)),
        "g_k": row(jnp.tile(p["g_k"][l], per)),
        "gn": (jnp.kron(jnp.eye(per, dtype=F32), jnp.ones((HEAD_DIM, HEAD_DIM), F32)) / HEAD_DIM).astype(BF16),
        "w_lconv": p["w_lru_conv"][l],
        "b_lconv": row(p["b_lru_conv"][l]),
        "w_ai": jnp.concatenate([_block_diag_groups(p["w_lru_a"][l]), _block_diag_groups(p["w_lru_i"][l])],
                                axis=2).astype(BF16),
        "b_a": row(p["b_lru_a"][l]),
        "b_i": row(p["b_lru_i"][l]),
        "lam": row(p["lru_lambda"][l]),
        "w_br_lru": p["w_br_lru"][l].astype(BF16),
        "sink_fill": {c: jnp.full((N_KV_HEADS, GROUP * c, MXU_TILE), NEG, F32).at[:, :, 0].set(
            jnp.repeat(sink, c, axis=1)) for c in chunks},
        "w_br_attn": w_ba.reshape(D_MODEL, D_MODEL).astype(BF16),
        "w_out": p["w_out"][l].astype(BF16),
        "g_ffn": row(p["g_ffn"][l]),
        "w_up": p["w_up"][l].astype(BF16),
        "w_fconv": p["w_ffn_conv"][l],
        "b_fconv": row(p["b_ffn_conv"][l]),
        "w_down": p["w_down"][l].astype(BF16),
    }


def _pick(n, pref):
    return pref if n % pref == 0 else n


def _layer(x, lw, conv_state, h_state, ffn_state, k_hist, v_hist, b, t_len, chunk, chained):
    m = b * t_len
    u, gy, q, g, k, v = _inproj(x, lw, _pick(m, 512))
    z, conv_new, h_new = _lru(u, gy, g, conv_state, h_state, lw, b, t_len, chained)
    if chained:
        k_hist, v_hist = k, v
    h, hn = _attn(q, k, v, k_hist, v_hist, z, g, x, lw, b, t_len, chunk, chained)
    y, ffn_new = _ffn(hn, h, ffn_state, lw, b, t_len, chained)
    return y, k, v, conv_new, h_new, ffn_new


def _front_pad(state):
    return jnp.pad(state, ((0, 0), (SUBLANES - state.shape[1], 0), (0, 0)))


def kernel(x_prompt, x_sample, state_lru_conv, state_lru_h, cache_k, cache_v, state_ffn_conv, g_mix, w_in, b_gate, w_lru_conv, b_lru_conv, w_lru_a, b_lru_a, w_lru_i, b_lru_i, lru_lambda, g_q, g_k, attn_sink, w_br_lru, w_br_attn, w_out, g_ffn, w_up, w_ffn_conv, b_ffn_conv, w_down):
    params = dict(g_mix=g_mix, w_in=w_in, b_gate=b_gate, w_lru_conv=w_lru_conv, b_lru_conv=b_lru_conv,
                  w_lru_a=w_lru_a, b_lru_a=b_lru_a, w_lru_i=w_lru_i, b_lru_i=b_lru_i, lru_lambda=lru_lambda,
                  g_q=g_q, g_k=g_k, attn_sink=attn_sink, w_br_lru=w_br_lru, w_br_attn=w_br_attn, w_out=w_out,
                  g_ffn=g_ffn, w_up=w_up, w_ffn_conv=w_ffn_conv, b_ffn_conv=b_ffn_conv, w_down=w_down)
    depth = w_in.shape[0]
    bp, sp, _ = x_prompt.shape
    bs, ss, _ = x_sample.shape
    p_chunk, s_chunk = 64, ss
    xp = x_prompt.reshape(bp * sp, D_MODEL)
    xs = x_sample.reshape(bs * ss, D_MODEL)
    zeros_p = (jnp.zeros((bp, SUBLANES, D_LRU), F32), jnp.zeros((bp, 1, D_LRU), F32),
               jnp.zeros((bp, SUBLANES, D_FF), F32))
    outs = {n: [] for n in ("p_lc", "p_lh", "p_k", "p_v", "p_fc", "s_lc", "s_lh", "s_k", "s_v", "s_fc")}
    for l in range(depth):
        lw = _prep_layer(l, params, {p_chunk, s_chunk})
        xp, k, v, lc, lh, fc = _layer(xp, lw, *zeros_p, None, None, bp, sp, p_chunk, True)
        outs["p_lc"].append(lc)
        outs["p_lh"].append(lh.reshape(bp, D_LRU))
        outs["p_k"].append(k.reshape(bp, sp, N_KV_HEADS, HEAD_DIM)[:, sp - WINDOW:])
        outs["p_v"].append(v.reshape(bp, sp, N_KV_HEADS, HEAD_DIM)[:, sp - WINDOW:])
        outs["p_fc"].append(fc)
        xs, k, v, lc, lh, fc = _layer(
            xs, lw, _front_pad(state_lru_conv[l]), state_lru_h[l], _front_pad(state_ffn_conv[l]),
            cache_k[l].reshape(bs * WINDOW, D_KV), cache_v[l].reshape(bs * WINDOW, D_KV),
            bs, ss, s_chunk, False)
        outs["s_lc"].append(lc.transpose(1, 0, 2))
        outs["s_lh"].append(lh.reshape(bs, D_LRU))
        outs["s_k"].append(k.reshape(bs, ss, N_KV_HEADS, HEAD_DIM))
        outs["s_v"].append(v.reshape(bs, ss, N_KV_HEADS, HEAD_DIM))
        outs["s_fc"].append(fc)
    st = {n: jnp.stack(v) for n, v in outs.items()}
    return (xp.reshape(bp, sp, D_MODEL), xs.reshape(bs, ss, D_MODEL),
            st["p_lc"], st["p_lh"], st["p_k"], st["p_v"], st["p_fc"],
            st["s_lc"], st["s_lh"], st["s_k"], st["s_v"], st["s_fc"])
```

```python
import functools

import jax
import jax.numpy as jnp
from jax import lax
from jax.experimental import pallas as pl
from jax.experimental.pallas import tpu as pltpu

D_MODEL = 1024
N_HEADS = 16
N_KV_HEADS = 4
HEAD_DIM = 64
GROUP = N_HEADS // N_KV_HEADS
D_KV = N_KV_HEADS * HEAD_DIM
WINDOW = 128
D_LRU = D_MODEL
N_LRU_BLOCKS = 16
LRU_BLOCK = D_LRU // N_LRU_BLOCKS
LRU_C = 8.0
LRU_CONV_W = 4
D_FF = 3 * D_MODEL
FFN_CONV_W = 3
EPS = 1e-6
NEG = -1e30

MXU_TILE = 256
SUBLANES = 8
LANES = 128
SEG = 32
SEG_PITCH = SEG + SUBLANES
TINY = 1e-30
N_GATE_GROUPS = D_LRU // MXU_TILE

_U0, _Y0, _Q0, _G0, _K0, _V0, _D_IN = 0, 1024, 2048, 3072, 5120, 5376, 5632

VMEM_LIMIT = 56 * 1024 * 1024

BF16 = jnp.bfloat16
F32 = jnp.float32


def _dot(a, b):
    return jnp.dot(a, b, preferred_element_type=F32)


def _rms_scale(x):
    return lax.rsqrt(jnp.mean(x * x, axis=-1, keepdims=True) + EPS)


def _const_spec(shape):
    nd = len(shape)
    return pl.BlockSpec(shape, lambda *_: (0,) * nd, pipeline_mode=pl.Buffered(1))


def _inproj_kernel(x_ref, gmix_ref, w_ref, bg_ref, gq_ref, gk_ref, gn_ref,
                   u_ref, gy_ref, q_ref, g_ref, k_ref, v_ref):
    x = x_ref[...]
    xn = (x * _rms_scale(x) * gmix_ref[...]).astype(BF16)

    def proj(lo, hi):
        return _dot(xn, w_ref[:, lo:hi])

    def head_norm(t, gain):
        ms = _dot((t * t).astype(BF16), gn_ref[...])
        return t * lax.rsqrt(ms + EPS) * gain

    u_ref[...] = proj(_U0, _Y0)
    gy_ref[...] = jax.nn.gelu(proj(_Y0, _Q0)).astype(BF16)
    q = proj(_Q0, _G0)
    kv = proj(_K0, _D_IN)
    scale = HEAD_DIM ** -0.5
    for j in range(D_MODEL // MXU_TILE):
        cols = slice(j * MXU_TILE, (j + 1) * MXU_TILE)
        q_ref[:, cols] = (head_norm(q[:, cols], gq_ref[...]) * scale).astype(BF16)
    g_ref[...] = jax.nn.sigmoid(proj(_G0, _K0) + bg_ref[...]).astype(BF16)
    k_ref[...] = head_norm(kv[:, :D_KV], gk_ref[...])
    v_ref[...] = kv[:, D_KV:]


def _inproj(x, lw, tm):
    m = x.shape[0]
    row = lambda w: pl.BlockSpec((tm, w), lambda i: (i, 0))
    return pl.pallas_call(
        _inproj_kernel,
        grid=(m // tm,),
        in_specs=[row(D_MODEL), _const_spec((1, D_MODEL)), _const_spec((D_MODEL, _D_IN)),
                  _const_spec((1, 2 * D_MODEL)), _const_spec((1, MXU_TILE)), _const_spec((1, MXU_TILE)),
                  _const_spec((MXU_TILE, MXU_TILE))],
        out_specs=[row(D_LRU), row(D_LRU), row(D_MODEL), row(2 * D_MODEL), row(D_KV), row(D_KV)],
        out_shape=[jax.ShapeDtypeStruct((m, D_LRU), F32), jax.ShapeDtypeStruct((m, D_LRU), BF16),
                   jax.ShapeDtypeStruct((m, D_MODEL), BF16), jax.ShapeDtypeStruct((m, 2 * D_MODEL), BF16),
                   jax.ShapeDtypeStruct((m, D_KV), F32), jax.ShapeDtypeStruct((m, D_KV), F32)],
        compiler_params=pltpu.CompilerParams(vmem_limit_bytes=VMEM_LIMIT),
        name="inproj",
    )(x, lw["g_mix"], lw["w_in"], lw["b_gate"], lw["g_q"], lw["g_k"], lw["gn"])


def _lru_kernel(u_ref, gy_ref, g0_ref, cs_ref, h0_ref, wc_ref, bc_ref, wai_ref, ba_ref, bi_ref,
                lam_ref, wbr_ref, z_ref, cso_ref, ho_ref,
                us, gys, brs, carry_u, carry_h, *, nseg, chained):
    t = pl.program_id(1)
    tm = nseg * SEG
    nlb = D_LRU // LANES
    taps = LRU_CONV_W - 1

    if chained:
        @pl.when(t == 0)
        def _():
            carry_u[...] = cs_ref[...]
            carry_h[...] = h0_ref[...]

    for lb in range(nlb):
        cols = slice(lb * LANES, (lb + 1) * LANES)
        for r in range(nseg):
            base = r * SEG_PITCH
            rows = slice(r * SEG, (r + 1) * SEG)
            if not chained:
                prev = cs_ref[r, :, cols]
            elif r == 0:
                prev = carry_u[:, cols]
            else:
                prev = u_ref[r * SEG - SUBLANES:r * SEG, cols]
            us[lb, base:base + SUBLANES, :] = prev
            us[lb, base + SUBLANES:base + SEG_PITCH, :] = u_ref[rows, cols]
            gys[lb, base + SUBLANES:base + SEG_PITCH, :] = gy_ref[rows, cols].astype(F32)

    def seg_rows(i):
        return pl.ds(SUBLANES + i, nseg, stride=SEG_PITCH)

    xc_lb, u_tail = [], []
    for lb in range(nlb):
        cols = slice(lb * LANES, (lb + 1) * LANES)
        w = [jnp.broadcast_to(wc_ref[k:k + 1, cols], (nseg, LANES)) for k in range(LRU_CONV_W)]
        bias = jnp.broadcast_to(bc_ref[:, cols], (nseg, LANES))
        ut = [us[lb, seg_rows(i - taps), :] for i in range(SEG + taps)]
        steps = []
        for j in range(SEG):
            x = bias
            for k in range(LRU_CONV_W):
                x = x + w[k] * ut[j + k]
            steps.append(x)
        xc_lb.append(jnp.concatenate(steps, axis=0))
        u_tail.append(ut[SEG:])
    xc = jnp.concatenate(xc_lb, axis=1)

    lam = lam_ref[...]
    softplus_neg_lam = jnp.maximum(-lam, 0.0) + jnp.log1p(jnp.exp(-jnp.abs(lam)))
    xcb = xc.astype(BF16)
    a_lb, b_lb = [], []
    for j in range(N_GATE_GROUPS):
        cols = slice(j * MXU_TILE, (j + 1) * MXU_TILE)
        ri = _dot(xcb[:, cols], wai_ref[j])
        r = jax.nn.sigmoid(ri[:, :MXU_TILE] + ba_ref[:, cols])
        i = jax.nn.sigmoid(ri[:, MXU_TILE:] + bi_ref[:, cols])
        a = jnp.exp(-LRU_C * r * softplus_neg_lam[:, cols])
        one_minus_a2 = 1.0 - a * a
        mult = one_minus_a2 * lax.rsqrt(jnp.maximum(one_minus_a2, TINY))
        b = mult * i * xc[:, cols]
        for k in range(MXU_TILE // LANES):
            a_lb.append(a[:, k * LANES:(k + 1) * LANES])
            b_lb.append(b[:, k * LANES:(k + 1) * LANES])

    hg_lb, h_last, c_next = [], [], []
    for lb in range(nlb):
        cols = slice(lb * LANES, (lb + 1) * LANES)
        step = lambda v, j: v[j * nseg:(j + 1) * nseg, :]
        if chained:
            hs, ap = jnp.zeros((nseg, LANES), F32), jnp.ones((nseg, LANES), F32)
            hs_l, ap_l = [], []
            for j in range(SEG):
                a = step(a_lb[lb], j)
                hs = a * hs + step(b_lb[lb], j)
                ap = a * ap
                hs_l.append(hs)
                ap_l.append(ap)
            c = carry_h[:, cols]
            cin = []
            for r in range(nseg):
                cin.append(c)
                c = ap[r:r + 1, :] * c + hs[r:r + 1, :]
            c_next.append(c)
            cin = jnp.concatenate(cin, axis=0)
            h_l = [hs_l[j] + ap_l[j] * cin for j in range(SEG)]
        else:
            h, h_l = h0_ref[:, cols], []
            for j in range(SEG):
                h = step(a_lb[lb], j) * h + step(b_lb[lb], j)
                h_l.append(h)
            h_last.append(h)
        hg_lb.append(jnp.concatenate(
            [h_l[j] * gys[lb, seg_rows(j), :] for j in range(SEG)], axis=0))
    hg = jnp.concatenate(hg_lb, axis=1).astype(BF16)

    br = _dot(hg, wbr_ref[...])
    for lb in range(nlb):
        cols = slice(lb * LANES, (lb + 1) * LANES)
        for j in range(SEG):
            brs[lb, seg_rows(j), :] = br[j * nseg:(j + 1) * nseg, cols]
    for lb in range(nlb):
        cols = slice(lb * LANES, (lb + 1) * LANES)
        for r in range(nseg):
            base = r * SEG_PITCH
            rows = slice(r * SEG, (r + 1) * SEG)
            z_ref[rows, cols] = g0_ref[rows, cols].astype(F32) * brs[lb, base + SUBLANES:base + SEG_PITCH, :]

    if chained:
        carry_u[...] = u_ref[tm - SUBLANES:tm, :]
        c = jnp.concatenate(c_next, axis=1)
        carry_h[...] = c

        @pl.when(t == pl.num_programs(1) - 1)
        def _():
            cso_ref[...] = u_ref[tm - taps:tm, :]
            ho_ref[...] = c
    else:
        for k in range(taps):
            cso_ref[k] = jnp.concatenate([u_tail[lb][k] for lb in range(nlb)], axis=1)
        ho_ref[...] = jnp.concatenate(h_last, axis=1)


def _lru(u, gy, g, conv_state, h_state, lw, b, t_len, chained):
    taps = LRU_CONV_W - 1
    if chained:
        nseg = SUBLANES
        tm = nseg * SEG
        nt = t_len // tm
        grid = (b, nt)
        row = lambda blk: pl.BlockSpec((tm, D_LRU), lambda bi, ti: (bi * nt + ti, blk))
        per_b = lambda r: pl.BlockSpec((None, r, D_LRU), lambda bi, ti: (bi, 0, 0))
        state_specs = [per_b(SUBLANES), per_b(1)]
        out_state_specs = [per_b(taps), per_b(1)]
        out_state_shapes = [jax.ShapeDtypeStruct((b, taps, D_LRU), F32), jax.ShapeDtypeStruct((b, 1, D_LRU), F32)]
    else:
        assert t_len == SEG
        nseg = SUBLANES if b % SUBLANES == 0 else b
        tm = nseg * SEG
        grid = (b // nseg, 1)
        row = lambda blk: pl.BlockSpec((tm, D_LRU), lambda bi, ti: (bi, blk))
        state_specs = [pl.BlockSpec((nseg, SUBLANES, D_LRU), lambda bi, ti: (bi, 0, 0)),
                       pl.BlockSpec((nseg, D_LRU), lambda bi, ti: (bi, 0))]
        out_state_specs = [pl.BlockSpec((taps, nseg, D_LRU), lambda bi, ti: (0, bi, 0)),
                           pl.BlockSpec((nseg, D_LRU), lambda bi, ti: (bi, 0))]
        out_state_shapes = [jax.ShapeDtypeStruct((taps, b, D_LRU), F32), jax.ShapeDtypeStruct((b, D_LRU), F32)]
    vec = _const_spec((1, D_LRU))
    slab = pltpu.VMEM((D_LRU // LANES, nseg * SEG_PITCH, LANES), F32)
    kern = functools.partial(_lru_kernel, nseg=nseg, chained=chained)
    return pl.pallas_call(
        kern,
        grid=grid,
        in_specs=[row(0), row(0), row(0)] + state_specs
                 + [_const_spec((LRU_CONV_W, D_LRU)), vec,
                    _const_spec((N_GATE_GROUPS, MXU_TILE, 2 * MXU_TILE)), vec, vec, vec,
                    _const_spec((D_LRU, D_MODEL))],
        out_specs=[row(0)] + out_state_specs,
        out_shape=[jax.ShapeDtypeStruct((b * t_len, D_MODEL), F32)] + out_state_shapes,
        scratch_shapes=[slab, slab, slab, pltpu.VMEM((SUBLANES, D_LRU), F32), pltpu.VMEM((1, D_LRU), F32)],
        compiler_params=pltpu.CompilerParams(vmem_limit_bytes=VMEM_LIMIT),
        name="lru",
    )(u, gy, g, conv_state, h_state, lw["w_lconv"], lw["b_lconv"], lw["w_ai"], lw["b_a"], lw["b_i"],
      lw["lam"], lw["w_br_lru"])


def _attn_kernel(q_ref, kh_ref, vh_ref, kt_ref, vt_ref, fill_ref, z_ref, g1_ref, x_ref,
                 wba_ref, wout_ref, gffn_ref, h_ref, hn_ref,
                 kbuf, vbuf, attn_s, *, tq, chunk, chained):
    ti = pl.program_id(1)
    nunits = tq // chunk
    win = WINDOW + chunk
    pad = MXU_TILE - win
    col_head = lax.broadcasted_iota(jnp.int32, (1, D_KV), 1) // HEAD_DIM
    for kh in range(N_KV_HEADS):
        m = (col_head == kh).astype(F32)
        for src_h, src_t, buf in ((kh_ref, kt_ref, kbuf), (vh_ref, vt_ref, vbuf)):
            if chained:
                buf[kh, 0:WINDOW, :] = (src_h[...] * m).astype(BF16)
                buf[kh, WINDOW:, :] = (src_t[...] * m).astype(BF16)
            else:
                for s in range(nunits):
                    buf[kh, s * win:s * win + WINDOW, :] = (
                        src_h[s * WINDOW:(s + 1) * WINDOW, :] * m).astype(BF16)
                    buf[kh, s * win + WINDOW:(s + 1) * win, :] = (
                        src_t[s * chunk:(s + 1) * chunk, :] * m).astype(BF16)

    zero_keys = jnp.zeros((pad, D_KV), BF16)
    key_pos = lax.broadcasted_iota(jnp.int32, (1, MXU_TILE), 1)
    for c in range(nunits):
        r0 = c * chunk
        k0 = r0 if chained else c * win
        qst = jnp.concatenate(
            [q_ref[r0:r0 + chunk, g * D_KV:(g + 1) * D_KV] for g in range(GROUP)], axis=0)

        def keys(buf):
            return jnp.concatenate(
                [blk for kh in range(N_KV_HEADS) for blk in (zero_keys, buf[kh, k0:k0 + win, :])], axis=0)

        s_all = lax.dot_general(qst, keys(kbuf), (((1,), (1,)), ((), ())), preferred_element_type=F32)
        valid = key_pos >= pad
        if chained and r0 < WINDOW:
            valid = jnp.logical_and(valid, jnp.logical_or(key_pos >= pad + WINDOW - r0, ti > 0))
        probs = []
        for kh in range(N_KV_HEADS):
            s = jnp.where(valid, s_all[:, kh * MXU_TILE:(kh + 1) * MXU_TILE], fill_ref[kh])
            e = jnp.exp(s - jnp.max(s, axis=-1, keepdims=True))
            probs.append((e * (1.0 / jnp.sum(e, axis=-1, keepdims=True))).astype(BF16))
        o = _dot(jnp.concatenate(probs, axis=1), keys(vbuf))
        for g in range(GROUP):
            attn_s[r0:r0 + chunk, g * D_KV:(g + 1) * D_KV] = o[g * chunk:(g + 1) * chunk, :].astype(BF16)

    br = _dot(attn_s[...], wba_ref[...])
    mix = (z_ref[...] + g1_ref[...].astype(F32) * br).astype(BF16)
    h = x_ref[...] + _dot(mix, wout_ref[...])
    h_ref[...] = h
    hn_ref[...] = (h * _rms_scale(h) * gffn_ref[...]).astype(BF16)


def _attn(q, k, v, k_hist, v_hist, z, g, x, lw, b, t_len, chunk, chained):
    if chained:
        tq = _pick(t_len, 256)
        nt = t_len // tq
        grid = (b, nt)
        row = lambda w, blk=0: pl.BlockSpec((tq, w), lambda bi, ti: (bi * nt + ti, blk))
        per_b, per_t = t_len // WINDOW, tq // WINDOW
        hist = pl.BlockSpec((WINDOW, D_KV), lambda bi, ti: (jnp.maximum(bi * per_b + ti * per_t - 1, 0), 0))
        kv_rows = WINDOW + tq
    else:
        assert t_len == chunk
        ns = SUBLANES if b % SUBLANES == 0 else b
        tq = ns * chunk
        grid = (b // ns, 1)
        row = lambda w, blk=0: pl.BlockSpec((tq, w), lambda bi, ti: (bi, blk))
        hist = pl.BlockSpec((ns * WINDOW, D_KV), lambda bi, ti: (bi, 0))
        kv_rows = ns * (WINDOW + chunk)
    kern = functools.partial(_attn_kernel, tq=tq, chunk=chunk, chained=chained)
    return pl.pallas_call(
        kern,
        grid=grid,
        in_specs=[row(D_MODEL), hist, hist, row(D_KV), row(D_KV),
                  _const_spec((N_KV_HEADS, GROUP * chunk, MXU_TILE)),
                  row(D_MODEL), row(D_MODEL, 1), row(D_MODEL),
                  _const_spec((D_MODEL, D_MODEL)), _const_spec((D_MODEL, D_MODEL)), _const_spec((1, D_MODEL))],
        out_specs=[row(D_MODEL), row(D_MODEL)],
        out_shape=[jax.ShapeDtypeStruct((b * t_len, D_MODEL), F32),
                   jax.ShapeDtypeStruct((b * t_len, D_MODEL), BF16)],
        scratch_shapes=[pltpu.VMEM((N_KV_HEADS, kv_rows, D_KV), BF16)] * 2
                       + [pltpu.VMEM((tq, D_MODEL), BF16)],
        compiler_params=pltpu.CompilerParams(vmem_limit_bytes=VMEM_LIMIT),
        name="attn",
    )(q, k_hist, v_hist, k, v, lw["sink_fill"][chunk], z, g, x, lw["w_br_attn"], lw["w_out"], lw["g_ffn"])


def _ffn_kernel(hn_ref, h_ref, fs_ref, wup_ref, wfc_ref, bfc_ref, wdn_ref, y_ref, fso_ref, gbuf,
                *, tm, nstreams, chained):
    t = pl.program_id(1)
    taps = FFN_CONV_W - 1
    seg = tm // nstreams
    pitch = seg + SUBLANES

    if chained:
        @pl.when(t == 0)
        def _():
            gbuf[0:SUBLANES, :] = fs_ref[...]

    hn = hn_ref[...]
    gate = _dot(hn, wup_ref[:, :D_FF])
    up = _dot(hn, wup_ref[:, D_FF:])
    parts = []
    for r in range(nstreams):
        base = r * pitch
        if not chained:
            gbuf[base:base + SUBLANES, :] = fs_ref[r]
        gbuf[base + SUBLANES:base + pitch, :] = gate[r * seg:(r + 1) * seg, :]
        gc = bfc_ref[...]
        for j in range(FFN_CONV_W):
            lo = base + SUBLANES - taps + j
            gc = gc + wfc_ref[j:j + 1, :] * gbuf[lo:lo + seg, :]
        parts.append(gc)
    gc = parts[0] if nstreams == 1 else jnp.concatenate(parts, axis=0)
    act = (jax.nn.gelu(gc) * up).astype(BF16)
    y_ref[...] = h_ref[...] + _dot(act, wdn_ref[...])

    if chained:
        gbuf[0:SUBLANES, :] = gbuf[tm:tm + SUBLANES, :]

        @pl.when(t == pl.num_programs(1) - 1)
        def _():
            fso_ref[...] = gbuf[SUBLANES - taps:SUBLANES, :]
    else:
        for r in range(nstreams):
            fso_ref[r] = gbuf[(r + 1) * pitch - taps:(r + 1) * pitch, :]


def _ffn(hn, h, ffn_state, lw, b, t_len, chained):
    taps = FFN_CONV_W - 1
    if chained:
        tm, ns = _pick(t_len, 256), 1
        nt = t_len // tm
        grid = (b, nt)
        row = pl.BlockSpec((tm, D_MODEL), lambda bi, ti: (bi * nt + ti, 0))
        state = lambda r: pl.BlockSpec((None, r, D_FF), lambda bi, ti: (bi, 0, 0))
    else:
        ns = SUBLANES if b % SUBLANES == 0 else b
        tm = ns * t_len
        grid = (b // ns, 1)
        row = pl.BlockSpec((tm, D_MODEL), lambda bi, ti: (bi, 0))
        state = lambda r: pl.BlockSpec((ns, r, D_FF), lambda bi, ti: (bi, 0, 0))
    kern = functools.partial(_ffn_kernel, tm=tm, nstreams=ns, chained=chained)
    return pl.pallas_call(
        kern,
        grid=grid,
        in_specs=[row, row, state(SUBLANES), _const_spec((D_MODEL, 2 * D_FF)),
                  _const_spec((FFN_CONV_W, D_FF)), _const_spec((1, D_FF)), _const_spec((D_FF, D_MODEL))],
        out_specs=[row, state(taps)],
        out_shape=[jax.ShapeDtypeStruct((b * t_len, D_MODEL), F32),
                   jax.ShapeDtypeStruct((b, taps, D_FF), F32)],
        scratch_shapes=[pltpu.VMEM((ns * (tm // ns + SUBLANES), D_FF), F32)],
        compiler_params=pltpu.CompilerParams(vmem_limit_bytes=VMEM_LIMIT),
        name="ffn",
    )(hn, h, ffn_state, lw["w_up"], lw["w_fconv"], lw["b_fconv"], lw["w_down"])


def _block_diag_groups(w):
    per = MXU_TILE // LRU_BLOCK
    w4 = w.reshape(N_GATE_GROUPS, per, LRU_BLOCK, LRU_BLOCK)
    return jnp.einsum("jncd,nm->jncmd", w4, jnp.eye(per, dtype=w.dtype)).reshape(
        N_GATE_GROUPS, MXU_TILE, MXU_TILE)


def _prep_layer(l, p, chunks):
    w_in = p["w_in"][l]
    wq = w_in[:, 2048:3072].reshape(D_MODEL, N_KV_HEADS, GROUP, HEAD_DIM).transpose(0, 2, 1, 3)
    w_in_p = jnp.concatenate(
        [w_in[:, :2048], wq.reshape(D_MODEL, D_MODEL), w_in[:, 3584:], w_in[:, 3072:3584]], axis=1)
    w_ba = p["w_br_attn"][l].reshape(N_KV_HEADS, GROUP, HEAD_DIM, D_MODEL).transpose(1, 0, 2, 3)
    per = MXU_TILE // HEAD_DIM
    sink = p["attn_sink"][l].reshape(N_KV_HEADS, GROUP)
    row = lambda v: v.reshape(1, -1)
    return {
        "g_mix": row(p["g_mix"][l]),
        "w_in": w_in_p.astype(BF16),
        "b_gate": row(p["b_gate"][l]),
        "g_q": row(jnp.tile(p["g_q"][l], per)),
        "g_k": row(jnp.tile(p["g_k"][l], per)),
        "gn": (jnp.kron(jnp.eye(per, dtype=F32), jnp.ones((HEAD_DIM, HEAD_DIM), F32)) / HEAD_DIM).astype(BF16),
        "w_lconv": p["w_lru_conv"][l],
        "b_lconv": row(p["b_lru_conv"][l]),
        "w_ai": jnp.concatenate([_block_diag_groups(p["w_lru_a"][l]), _block_diag_groups(p["w_lru_i"][l])],
                                axis=2).astype(BF16),
        "b_a": row(p["b_lru_a"][l]),
        "b_i": row(p["b_lru_i"][l]),
        "lam": row(p["lru_lambda"][l]),
        "w_br_lru": p["w_br_lru"][l].astype(BF16),
        "sink_fill": {c: jnp.full((N_KV_HEADS, GROUP * c, MXU_TILE), NEG, F32).at[:, :, 0].set(
            jnp.repeat(sink, c, axis=1)) for c in chunks},
        "w_br_attn": w_ba.reshape(D_MODEL, D_MODEL).astype(BF16),
        "w_out": p["w_out"][l].astype(BF16),
        "g_ffn": row(p["g_ffn"][l]),
        "w_up": p["w_up"][l].astype(BF16),
        "w_fconv": p["w_ffn_conv"][l],
        "b_fconv": row(p["b_ffn_conv"][l]),
        "w_down": p["w_down"][l].astype(BF16),
    }


def _pick(n, pref):
    return pref if n % pref == 0 else n


def _layer(x, lw, conv_state, h_state, ffn_state, k_hist, v_hist, b, t_len, chunk, chained):
    m = b * t_len
    u, gy, q, g, k, v = _inproj(x, lw, _pick(m, 512))
    z, conv_new, h_new = _lru(u, gy, g, conv_state, h_state, lw, b, t_len, chained)
    if chained:
        k_hist, v_hist = k, v
    h, hn = _attn(q, k, v, k_hist, v_hist, z, g, x, lw, b, t_len, chunk, chained)
    y, ffn_new = _ffn(hn, h, ffn_state, lw, b, t_len, chained)
    return y, k, v, conv_new, h_new, ffn_new


def _last_window(kv, b, t_len):
    return kv.reshape(b, t_len, D_KV)[:, t_len - WINDOW:].reshape(b, WINDOW, N_KV_HEADS, HEAD_DIM)


def _front_pad(state):
    return jnp.pad(state, ((0, 0), (SUBLANES - state.shape[1], 0), (0, 0)))


def kernel(x_prompt, x_sample, state_lru_conv, state_lru_h, cache_k, cache_v, state_ffn_conv, g_mix, w_in, b_gate, w_lru_conv, b_lru_conv, w_lru_a, b_lru_a, w_lru_i, b_lru_i, lru_lambda, g_q, g_k, attn_sink, w_br_lru, w_br_attn, w_out, g_ffn, w_up, w_ffn_conv, b_ffn_conv, w_down):
    params = dict(g_mix=g_mix, w_in=w_in, b_gate=b_gate, w_lru_conv=w_lru_conv, b_lru_conv=b_lru_conv,
                  w_lru_a=w_lru_a, b_lru_a=b_lru_a, w_lru_i=w_lru_i, b_lru_i=b_lru_i, lru_lambda=lru_lambda,
                  g_q=g_q, g_k=g_k, attn_sink=attn_sink, w_br_lru=w_br_lru, w_br_attn=w_br_attn, w_out=w_out,
                  g_ffn=g_ffn, w_up=w_up, w_ffn_conv=w_ffn_conv, b_ffn_conv=b_ffn_conv, w_down=w_down)
    depth = w_in.shape[0]
    bp, sp, _ = x_prompt.shape
    bs, ss, _ = x_sample.shape
    p_chunk, s_chunk = 64, ss
    xp = x_prompt.reshape(bp * sp, D_MODEL)
    xs = x_sample.reshape(bs * ss, D_MODEL)
    zeros_p = (jnp.zeros((bp, SUBLANES, D_LRU), F32), jnp.zeros((bp, 1, D_LRU), F32),
               jnp.zeros((bp, SUBLANES, D_FF), F32))
    outs = {n: [] for n in ("p_lc", "p_lh", "p_k", "p_v", "p_fc", "s_lc", "s_lh", "s_k", "s_v", "s_fc")}
    for l in range(depth):
        lw = _prep_layer(l, params, {p_chunk, s_chunk})
        xp, k, v, lc, lh, fc = _layer(xp, lw, *zeros_p, None, None, bp, sp, p_chunk, True)
        outs["p_lc"].append(lc)
        outs["p_lh"].append(lh.reshape(bp, D_LRU))
        outs["p_k"].append(_last_window(k, bp, sp))
        outs["p_v"].append(_last_window(v, bp, sp))
        outs["p_fc"].append(fc)
        xs, k, v, lc, lh, fc = _layer(
            xs, lw, _front_pad(state_lru_conv[l]), state_lru_h[l], _front_pad(state_ffn_conv[l]),
            cache_k[l].reshape(bs * WINDOW, D_KV), cache_v[l].reshape(bs * WINDOW, D_KV),
            bs, ss, s_chunk, False)
        outs["s_lc"].append(lc.transpose(1, 0, 2))
        outs["s_lh"].append(lh.reshape(bs, D_LRU))
        outs["s_k"].append(k.reshape(bs, ss, N_KV_HEADS, HEAD_DIM))
        outs["s_v"].append(v.reshape(bs, ss, N_KV_HEADS, HEAD_DIM))
        outs["s_fc"].append(fc)
    st = {n: jnp.stack(v) for n, v in outs.items()}
    return (xp.reshape(bp, sp, D_MODEL), xs.reshape(bs, ss, D_MODEL),
            st["p_lc"], st["p_lh"], st["p_k"], st["p_v"], st["p_fc"],
            st["s_lc"], st["s_lh"], st["s_k"], st["s_v"], st["s_fc"])
```

```python
import functools

import jax
import jax.numpy as jnp
from jax import lax
from jax.experimental import pallas as pl
from jax.experimental.pallas import tpu as pltpu

D_MODEL = 1024
N_HEADS = 16
N_KV_HEADS = 4
HEAD_DIM = 64
GROUP = N_HEADS // N_KV_HEADS
D_KV = N_KV_HEADS * HEAD_DIM
WINDOW = 128
D_LRU = D_MODEL
N_LRU_BLOCKS = 16
LRU_BLOCK = D_LRU // N_LRU_BLOCKS
LRU_C = 8.0
LRU_CONV_W = 4
D_FF = 3 * D_MODEL
FFN_CONV_W = 3
EPS = 1e-6
NEG = -1e30

MXU_TILE = 256
SUBLANES = 8
LANES = 128
SEG = 32
SEG_PITCH = SEG + SUBLANES
TINY = 1e-30
LOG2_E = 1.4426950408889634
N_GATE_GROUPS = D_LRU // MXU_TILE

_U0, _Y0, _Q0, _G0, _K0, _V0, _D_IN = 0, 1024, 2048, 3072, 5120, 5376, 5632

VMEM_LIMIT = 56 * 1024 * 1024

BF16 = jnp.bfloat16
F32 = jnp.float32


def _dot(a, b):
    return jnp.dot(a, b, preferred_element_type=F32)


def _rms_scale(x):
    return lax.rsqrt(jnp.mean(x * x, axis=-1, keepdims=True) + EPS)


def _const_spec(shape):
    nd = len(shape)
    return pl.BlockSpec(shape, lambda *_: (0,) * nd, pipeline_mode=pl.Buffered(1))


def _head_norm(t, gain, gn):
    ms = _dot((t * t).astype(BF16), gn)
    return t * lax.rsqrt(ms + EPS) * gain


def _lru_body(u, gy, cs_ref, h0_ref, wc_ref, bc_ref, wai_ref, ba_ref, bi_ref, lam_ref, wbr_ref,
              us, gys, brs, carry_u, carry_h, *, nseg, chained):
    nlb = D_LRU // LANES
    taps = LRU_CONV_W - 1

    for lb in range(nlb):
        cols = slice(lb * LANES, (lb + 1) * LANES)
        for r in range(nseg):
            base = r * SEG_PITCH
            rows = slice(r * SEG, (r + 1) * SEG)
            if not chained:
                prev = cs_ref[r, :, cols]
            elif r == 0:
                prev = carry_u[:, cols]
            else:
                prev = u[r * SEG - SUBLANES:r * SEG, cols]
            us[lb, base:base + SUBLANES, :] = prev
            us[lb, base + SUBLANES:base + SEG_PITCH, :] = u[rows, cols]
            gys[lb, base + SUBLANES:base + SEG_PITCH, :] = gy[rows, cols]

    def seg_rows(i):
        return pl.ds(SUBLANES + i, nseg, stride=SEG_PITCH)

    xc_lb, u_tail = [], []
    for lb in range(nlb):
        cols = slice(lb * LANES, (lb + 1) * LANES)
        w = [wc_ref[k, 0:nseg, cols] for k in range(LRU_CONV_W)]
        bias = bc_ref[0:nseg, cols]
        ut = [us[lb, seg_rows(i - taps), :] for i in range(SEG + taps)]
        steps = []
        for j in range(SEG):
            x = bias
            for k in range(LRU_CONV_W):
                x = x + w[k] * ut[j + k]
            steps.append(x)
        xc_lb.append(jnp.concatenate(steps, axis=0))
        u_tail.append(ut[SEG:])
    xc = jnp.concatenate(xc_lb, axis=1)

    lam = lam_ref[...]
    softplus_neg_lam = jnp.maximum(-lam, 0.0) + jnp.log1p(jnp.exp(-jnp.abs(lam)))
    log2_a_per_r = (-LRU_C * LOG2_E) * softplus_neg_lam
    xcb = xc.astype(BF16)
    a_lb, b_lb = [], []
    for j in range(N_GATE_GROUPS):
        cols = slice(j * MXU_TILE, (j + 1) * MXU_TILE)
        ri = _dot(xcb[:, cols], wai_ref[j])
        r = jax.nn.sigmoid(ri[:, :MXU_TILE] + ba_ref[:, cols])
        i = jax.nn.sigmoid(ri[:, MXU_TILE:] + bi_ref[:, cols])
        a = jnp.exp2(r * log2_a_per_r[:, cols])
        one_minus_a2 = 1.0 - a * a
        mult = one_minus_a2 * lax.rsqrt(jnp.maximum(one_minus_a2, TINY))
        b = mult * i * xc[:, cols]
        for k in range(MXU_TILE // LANES):
            a_lb.append(a[:, k * LANES:(k + 1) * LANES])
            b_lb.append(b[:, k * LANES:(k + 1) * LANES])

    hg_lb, h_last, c_next = [], [], []
    for lb in range(nlb):
        cols = slice(lb * LANES, (lb + 1) * LANES)
        step = lambda v, j: v[j * nseg:(j + 1) * nseg, :]
        if chained:
            hs, ap = jnp.zeros((nseg, LANES), F32), jnp.ones((nseg, LANES), F32)
            hs_l, ap_l = [], []
            for j in range(SEG):
                a = step(a_lb[lb], j)
                hs = a * hs + step(b_lb[lb], j)
                ap = a * ap
                hs_l.append(hs)
                ap_l.append(ap)
            c = carry_h[:, cols]
            cin = []
            for r in range(nseg):
                cin.append(c)
                c = ap[r:r + 1, :] * c + hs[r:r + 1, :]
            c_next.append(c)
            cin = jnp.concatenate(cin, axis=0)
            h_l = [hs_l[j] + ap_l[j] * cin for j in range(SEG)]
        else:
            h, h_l = h0_ref[:, cols], []
            for j in range(SEG):
                h = step(a_lb[lb], j) * h + step(b_lb[lb], j)
                h_l.append(h)
            h_last.append(h)
        hg_lb.append(jnp.concatenate(
            [h_l[j] * gys[lb, seg_rows(j), :] for j in range(SEG)], axis=0))
    hg = jnp.concatenate(hg_lb, axis=1).astype(BF16)

    br = _dot(hg, wbr_ref[...])
    for lb in range(nlb):
        cols = slice(lb * LANES, (lb + 1) * LANES)
        for j in range(SEG):
            brs[lb, seg_rows(j), :] = br[j * nseg:(j + 1) * nseg, cols]
    br_nat = jnp.concatenate(
        [jnp.concatenate([brs[lb, r * SEG_PITCH + SUBLANES:(r + 1) * SEG_PITCH, :] for r in range(nseg)], axis=0)
         for lb in range(nlb)], axis=1)

    if chained:
        return br_nat, None, jnp.concatenate(c_next, axis=1)
    conv_tail = [jnp.concatenate([u_tail[lb][k] for lb in range(nlb)], axis=1) for k in range(taps)]
    return br_nat, conv_tail, jnp.concatenate(h_last, axis=1)


def _proj_lru_kernel(x_ref, gmix_ref, w_ref, bg_ref, gq_ref, gk_ref, gn_ref,
                     cs_ref, h0_ref, wc_ref, bc_ref, wai_ref, ba_ref, bi_ref, lam_ref, wbr_ref,
                     z_ref, q_ref, g1_ref, k_ref, v_ref, cso_ref, ho_ref,
                     us, gys, brs, carry_u, carry_h, *, nseg, chained):
    t = pl.program_id(1)
    tm = nseg * SEG
    taps = LRU_CONV_W - 1
    if chained:
        @pl.when(t == 0)
        def _():
            carry_u[...] = cs_ref[...]
            carry_h[...] = h0_ref[...]

    x = x_ref[...]
    xn = (x * _rms_scale(x) * gmix_ref[...]).astype(BF16)

    def proj(lo, hi):
        return _dot(xn, w_ref[:, lo:hi])

    u = proj(_U0, _Y0)
    gy = jax.nn.gelu(proj(_Y0, _Q0))
    br, conv_tail, h_end = _lru_body(
        u, gy, cs_ref, h0_ref, wc_ref, bc_ref, wai_ref, ba_ref, bi_ref, lam_ref, wbr_ref,
        us, gys, brs, carry_u, carry_h, nseg=nseg, chained=chained)
    gates = jax.nn.sigmoid(proj(_G0, _K0) + bg_ref[...])
    z_ref[...] = gates[:, :D_MODEL] * br
    g1_ref[...] = gates[:, D_MODEL:].astype(BF16)

    q = proj(_Q0, _G0)
    kv = proj(_K0, _D_IN)
    scale = HEAD_DIM ** -0.5
    for j in range(D_MODEL // MXU_TILE):
        cols = slice(j * MXU_TILE, (j + 1) * MXU_TILE)
        q_ref[:, cols] = (_head_norm(q[:, cols], gq_ref[...], gn_ref[...]) * scale).astype(BF16)
    k_ref[...] = _head_norm(kv[:, :D_KV], gk_ref[...], gn_ref[...])
    v_ref[...] = kv[:, D_KV:]

    if chained:
        carry_u[...] = u[tm - SUBLANES:tm, :]
        carry_h[...] = h_end

        @pl.when(t == pl.num_programs(1) - 1)
        def _():
            cso_ref[...] = u[tm - taps:tm, :]
            ho_ref[...] = h_end
    else:
        for k in range(taps):
            cso_ref[k] = conv_tail[k]
        ho_ref[...] = h_end


def _proj_lru(x, conv_state, h_state, lw, b, t_len, chained):
    taps = LRU_CONV_W - 1
    if chained:
        nseg = SUBLANES
        tm = nseg * SEG
        nt = t_len // tm
        grid = (b, nt)
        row = lambda w: pl.BlockSpec((tm, w), lambda bi, ti: (bi * nt + ti, 0))
        per_b = lambda r: pl.BlockSpec((None, r, D_LRU), lambda bi, ti: (bi, 0, 0))
        state_specs = [per_b(SUBLANES), per_b(1)]
        out_state_specs = [per_b(taps), per_b(1)]
        out_state_shapes = [jax.ShapeDtypeStruct((b, taps, D_LRU), F32), jax.ShapeDtypeStruct((b, 1, D_LRU), F32)]
    else:
        assert t_len == SEG
        nseg = SUBLANES if b % SUBLANES == 0 else b
        tm = nseg * SEG
        grid = (b // nseg, 1)
        row = lambda w: pl.BlockSpec((tm, w), lambda bi, ti: (bi, 0))
        state_specs = [pl.BlockSpec((nseg, SUBLANES, D_LRU), lambda bi, ti: (bi, 0, 0)),
                       pl.BlockSpec((nseg, D_LRU), lambda bi, ti: (bi, 0))]
        out_state_specs = [pl.BlockSpec((taps, nseg, D_LRU), lambda bi, ti: (0, bi, 0)),
                           pl.BlockSpec((nseg, D_LRU), lambda bi, ti: (bi, 0))]
        out_state_shapes = [jax.ShapeDtypeStruct((taps, b, D_LRU), F32), jax.ShapeDtypeStruct((b, D_LRU), F32)]
    m = b * t_len
    vec = _const_spec((1, D_LRU))
    slab = pltpu.VMEM((D_LRU // LANES, nseg * SEG_PITCH, LANES), F32)
    kern = functools.partial(_proj_lru_kernel, nseg=nseg, chained=chained)
    return pl.pallas_call(
        kern,
        grid=grid,
        in_specs=[row(D_MODEL), _const_spec((1, D_MODEL)), _const_spec((D_MODEL, _D_IN)),
                  _const_spec((1, 2 * D_MODEL)), _const_spec((1, MXU_TILE)), _const_spec((1, MXU_TILE)),
                  _const_spec((MXU_TILE, MXU_TILE))] + state_specs
                 + [_const_spec((LRU_CONV_W, SUBLANES, D_LRU)), _const_spec((SUBLANES, D_LRU)),
                    _const_spec((N_GATE_GROUPS, MXU_TILE, 2 * MXU_TILE)), vec, vec, vec,
                    _const_spec((D_LRU, D_MODEL))],
        out_specs=[row(D_MODEL), row(D_MODEL), row(D_MODEL), row(D_KV), row(D_KV)] + out_state_specs,
        out_shape=[jax.ShapeDtypeStruct((m, D_MODEL), F32), jax.ShapeDtypeStruct((m, D_MODEL), BF16),
                   jax.ShapeDtypeStruct((m, D_MODEL), BF16), jax.ShapeDtypeStruct((m, D_KV), F32),
                   jax.ShapeDtypeStruct((m, D_KV), F32)] + out_state_shapes,
        scratch_shapes=[slab, slab, slab, pltpu.VMEM((SUBLANES, D_LRU), F32), pltpu.VMEM((1, D_LRU), F32)],
        compiler_params=pltpu.CompilerParams(vmem_limit_bytes=VMEM_LIMIT),
        name="proj_lru",
    )(x, lw["g_mix"], lw["w_in"], lw["b_gate"], lw["g_q"], lw["g_k"], lw["gn"], conv_state, h_state,
      lw["w_lconv"], lw["b_lconv"], lw["w_ai"], lw["b_a"], lw["b_i"], lw["lam"], lw["w_br_lru"])


def _attn_ffn_kernel(q_ref, kh_ref, vh_ref, kt_ref, vt_ref, fill_ref, z_ref, g1_ref, x_ref,
                     wba_ref, wout_ref, gffn_ref, fs_ref, wup_ref, wfc_ref, bfc_ref, wdn_ref,
                     y_ref, fso_ref, kbuf, vbuf, attn_s, gbuf, *, tq, chunk, chained):
    ti = pl.program_id(1)
    nunits = tq // chunk
    win = WINDOW + chunk
    pad = MXU_TILE - win
    col_head = lax.broadcasted_iota(jnp.int32, (1, D_KV), 1) // HEAD_DIM
    for src_h, src_t, buf in ((kh_ref, kt_ref, kbuf), (vh_ref, vt_ref, vbuf)):
        hist, rows = src_h[...].astype(BF16), src_t[...].astype(BF16)
        for kh in range(N_KV_HEADS):
            m = (col_head == kh).astype(BF16)
            if chained:
                buf[kh, 0:WINDOW, :] = hist * m
                buf[kh, WINDOW:, :] = rows * m
            else:
                for s in range(nunits):
                    buf[kh, s * win:s * win + WINDOW, :] = hist[s * WINDOW:(s + 1) * WINDOW, :] * m
                    buf[kh, s * win + WINDOW:(s + 1) * win, :] = rows[s * chunk:(s + 1) * chunk, :] * m

    zero_keys = jnp.zeros((pad, D_KV), BF16)
    key_pos = lax.broadcasted_iota(jnp.int32, (1, MXU_TILE), 1)
    for c in range(nunits):
        r0 = c * chunk
        k0 = r0 if chained else c * win
        qst = jnp.concatenate(
            [q_ref[r0:r0 + chunk, g * D_KV:(g + 1) * D_KV] for g in range(GROUP)], axis=0)

        def keys(buf):
            return jnp.concatenate(
                [blk for kh in range(N_KV_HEADS) for blk in (zero_keys, buf[kh, k0:k0 + win, :])], axis=0)

        s_all = lax.dot_general(qst, keys(kbuf), (((1,), (1,)), ((), ())), preferred_element_type=F32)
        valid = key_pos >= pad
        if chained and r0 < WINDOW:
            valid = jnp.logical_and(valid, jnp.logical_or(key_pos >= pad + WINDOW - r0, ti > 0))
        probs = []
        for kh in range(N_KV_HEADS):
            s = jnp.where(valid, s_all[:, kh * MXU_TILE:(kh + 1) * MXU_TILE], fill_ref[kh])
            e = jnp.exp(s - jnp.max(s, axis=-1, keepdims=True))
            probs.append((e * (1.0 / jnp.sum(e, axis=-1, keepdims=True))).astype(BF16))
        o = _dot(jnp.concatenate(probs, axis=1), keys(vbuf))
        for g in range(GROUP):
            attn_s[r0:r0 + chunk, g * D_KV:(g + 1) * D_KV] = o[g * chunk:(g + 1) * chunk, :].astype(BF16)

    br = _dot(attn_s[...], wba_ref[...])
    mix = (z_ref[...] + g1_ref[...].astype(F32) * br).astype(BF16)
    h = x_ref[...] + _dot(mix, wout_ref[...])
    hn = (h * _rms_scale(h) * gffn_ref[...]).astype(BF16)

    taps = FFN_CONV_W - 1
    nstreams = 1 if chained else nunits
    seg = tq // nstreams
    pitch = seg + SUBLANES
    if chained:
        @pl.when(ti == 0)
        def _():
            gbuf[0:SUBLANES, :] = fs_ref[...]
    gate = _dot(hn, wup_ref[:, :D_FF])
    up = _dot(hn, wup_ref[:, D_FF:])
    parts = []
    for r in range(nstreams):
        base = r * pitch
        if not chained:
            gbuf[base:base + SUBLANES, :] = fs_ref[r]
        gbuf[base + SUBLANES:base + pitch, :] = gate[r * seg:(r + 1) * seg, :]
        gc = bfc_ref[...]
        for j in range(FFN_CONV_W):
            lo = base + SUBLANES - taps + j
            gc = gc + wfc_ref[j:j + 1, :] * gbuf[lo:lo + seg, :]
        parts.append(gc)
    gc = parts[0] if nstreams == 1 else jnp.concatenate(parts, axis=0)
    act = (jax.nn.gelu(gc) * up).astype(BF16)
    y_ref[...] = h + _dot(act, wdn_ref[...])

    if chained:
        gbuf[0:SUBLANES, :] = gbuf[tq:tq + SUBLANES, :]

        @pl.when(ti == pl.num_programs(1) - 1)
        def _():
            fso_ref[...] = gbuf[SUBLANES - taps:SUBLANES, :]
    else:
        for r in range(nstreams):
            fso_ref[r] = gbuf[(r + 1) * pitch - taps:(r + 1) * pitch, :]


def _attn_ffn(q, k, v, k_hist, v_hist, z, g1, x, ffn_state, lw, b, t_len, chunk, chained):
    taps = FFN_CONV_W - 1
    if chained:
        tq, ns = _pick(t_len, 256), 1
        nt = t_len // tq
        grid = (b, nt)
        row = lambda w: pl.BlockSpec((tq, w), lambda bi, ti: (bi * nt + ti, 0))
        per_b, per_t = t_len // WINDOW, tq // WINDOW
        hist = pl.BlockSpec((WINDOW, D_KV), lambda bi, ti: (jnp.maximum(bi * per_b + ti * per_t - 1, 0), 0))
        kv_rows = WINDOW + tq
        state = lambda r: pl.BlockSpec((None, r, D_FF), lambda bi, ti: (bi, 0, 0))
    else:
        assert t_len == chunk
        ns = SUBLANES if b % SUBLANES == 0 else b
        tq = ns * chunk
        grid = (b // ns, 1)
        row = lambda w: pl.BlockSpec((tq, w), lambda bi, ti: (bi, 0))
        hist = pl.BlockSpec((ns * WINDOW, D_KV), lambda bi, ti: (bi, 0))
        kv_rows = ns * (WINDOW + chunk)
        state = lambda r: pl.BlockSpec((ns, r, D_FF), lambda bi, ti: (bi, 0, 0))
    kern = functools.partial(_attn_ffn_kernel, tq=tq, chunk=chunk, chained=chained)
    return pl.pallas_call(
        kern,
        grid=grid,
        in_specs=[row(D_MODEL), hist, hist, row(D_KV), row(D_KV),
                  _const_spec((N_KV_HEADS, GROUP * chunk, MXU_TILE)),
                  row(D_MODEL), row(D_MODEL), row(D_MODEL),
                  _const_spec((D_MODEL, D_MODEL)), _const_spec((D_MODEL, D_MODEL)), _const_spec((1, D_MODEL)),
                  state(SUBLANES), _const_spec((D_MODEL, 2 * D_FF)),
                  _const_spec((FFN_CONV_W, D_FF)), _const_spec((1, D_FF)), _const_spec((D_FF, D_MODEL))],
        out_specs=[row(D_MODEL), state(taps)],
        out_shape=[jax.ShapeDtypeStruct((b * t_len, D_MODEL), F32),
                   jax.ShapeDtypeStruct((b, taps, D_FF), F32)],
        scratch_shapes=[pltpu.VMEM((N_KV_HEADS, kv_rows, D_KV), BF16)] * 2
                       + [pltpu.VMEM((tq, D_MODEL), BF16),
                          pltpu.VMEM((ns * (tq // ns + SUBLANES), D_FF), F32)],
        compiler_params=pltpu.CompilerParams(vmem_limit_bytes=VMEM_LIMIT),
        name="attn_ffn",
    )(q, k_hist, v_hist, k, v, lw["sink_fill"][chunk], z, g1, x, lw["w_br_attn"], lw["w_out"], lw["g_ffn"],
      ffn_state, lw["w_up"], lw["w_fconv"], lw["b_fconv"], lw["w_down"])


def _block_diag_groups(w):
    per = MXU_TILE // LRU_BLOCK
    w4 = w.reshape(N_GATE_GROUPS, per, LRU_BLOCK, LRU_BLOCK)
    return jnp.einsum("jncd,nm->jncmd", w4, jnp.eye(per, dtype=w.dtype)).reshape(
        N_GATE_GROUPS, MXU_TILE, MXU_TILE)


def _prep_layer(l, p, chunks):
    w_in = p["w_in"][l]
    wq = w_in[:, 2048:3072].reshape(D_MODEL, N_KV_HEADS, GROUP, HEAD_DIM).transpose(0, 2, 1, 3)
    w_in_p = jnp.concatenate(
        [w_in[:, :2048], wq.reshape(D_MODEL, D_MODEL), w_in[:, 3584:], w_in[:, 3072:3584]], axis=1)
    w_ba = p["w_br_attn"][l].reshape(N_KV_HEADS, GROUP, HEAD_DIM, D_MODEL).transpose(1, 0, 2, 3)
    per = MXU_TILE // HEAD_DIM
    sink = p["attn_sink"][l].reshape(N_KV_HEADS, GROUP)
    row = lambda v: v.reshape(1, -1)
    return {
        "g_mix": row(p["g_mix"][l]),
        "w_in": w_in_p.astype(BF16),
        "b_gate": row(p["b_gate"][l]),
        "g_q": row(jnp.tile(p["g_q"][l], per)),
        "g_k": row(jnp.tile(p["g_k"][l], per)),
        "gn": (jnp.kron(jnp.eye(per, dtype=F32), jnp.ones((HEAD_DIM, HEAD_DIM), F32)) / HEAD_DIM).astype(BF16),
        "w_lconv": jnp.broadcast_to(p["w_lru_conv"][l][:, None, :], (LRU_CONV_W, SUBLANES, D_LRU)),
        "b_lconv": jnp.broadcast_to(p["b_lru_conv"][l][None, :], (SUBLANES, D_LRU)),
        "w_ai": jnp.concatenate([_block_diag_groups(p["w_lru_a"][l]), _block_diag_groups(p["w_lru_i"][l])],
                                axis=2).astype(BF16),
        "b_a": row(p["b_lru_a"][l]),
        "b_i": row(p["b_lru_i"][l]),
        "lam": row(p["lru_lambda"][l]),
        "w_br_lru": p["w_br_lru"][l].astype(BF16),
        "sink_fill": {c: jnp.full((N_KV_HEADS, GROUP * c, MXU_TILE), NEG, F32).at[:, :, 0].set(
            jnp.repeat(sink, c, axis=1)) for c in chunks},
        "w_br_attn": w_ba.reshape(D_MODEL, D_MODEL).astype(BF16),
        "w_out": p["w_out"][l].astype(BF16),
        "g_ffn": row(p["g_ffn"][l]),
        "w_up": p["w_up"][l].astype(BF16),
        "w_fconv": p["w_ffn_conv"][l],
        "b_fconv": row(p["b_ffn_conv"][l]),
        "w_down": p["w_down"][l].astype(BF16),
    }


def _pick(n, pref):
    return pref if n % pref == 0 else n


def _layer(x, lw, conv_state, h_state, ffn_state, k_hist, v_hist, b, t_len, chunk, chained):
    z, q, g1, k, v, conv_new, h_new = _proj_lru(x, conv_state, h_state, lw, b, t_len, chained)
    if chained:
        k_hist, v_hist = k, v
    y, ffn_new = _attn_ffn(q, k, v, k_hist, v_hist, z, g1, x, ffn_state, lw, b, t_len, chunk, chained)
    return y, k, v, conv_new, h_new, ffn_new


def _last_window(kv, b, t_len):
    return kv.reshape(b, t_len, D_KV)[:, t_len - WINDOW:].reshape(b, WINDOW, N_KV_HEADS, HEAD_DIM)


def _front_pad(state):
    return jnp.pad(state, ((0, 0), (SUBLANES - state.shape[1], 0), (0, 0)))


def kernel(x_prompt, x_sample, state_lru_conv, state_lru_h, cache_k, cache_v, state_ffn_conv, g_mix, w_in, b_gate, w_lru_conv, b_lru_conv, w_lru_a, b_lru_a, w_lru_i, b_lru_i, lru_lambda, g_q, g_k, attn_sink, w_br_lru, w_br_attn, w_out, g_ffn, w_up, w_ffn_conv, b_ffn_conv, w_down):
    params = dict(g_mix=g_mix, w_in=w_in, b_gate=b_gate, w_lru_conv=w_lru_conv, b_lru_conv=b_lru_conv,
                  w_lru_a=w_lru_a, b_lru_a=b_lru_a, w_lru_i=w_lru_i, b_lru_i=b_lru_i, lru_lambda=lru_lambda,
                  g_q=g_q, g_k=g_k, attn_sink=attn_sink, w_br_lru=w_br_lru, w_br_attn=w_br_attn, w_out=w_out,
                  g_ffn=g_ffn, w_up=w_up, w_ffn_conv=w_ffn_conv, b_ffn_conv=b_ffn_conv, w_down=w_down)
    depth = w_in.shape[0]
    bp, sp, _ = x_prompt.shape
    bs, ss, _ = x_sample.shape
    p_chunk, s_chunk = 64, ss
    xp = x_prompt.reshape(bp * sp, D_MODEL)
    xs = x_sample.reshape(bs * ss, D_MODEL)
    zeros_p = (jnp.zeros((bp, SUBLANES, D_LRU), F32), jnp.zeros((bp, 1, D_LRU), F32),
               jnp.zeros((bp, SUBLANES, D_FF), F32))
    outs = {n: [] for n in ("p_lc", "p_lh", "p_k", "p_v", "p_fc", "s_lc", "s_lh", "s_k", "s_v", "s_fc")}
    for l in range(depth):
        lw = _prep_layer(l, params, {p_chunk, s_chunk})
        xp, k, v, lc, lh, fc = _layer(xp, lw, *zeros_p, None, None, bp, sp, p_chunk, True)
        outs["p_lc"].append(lc)
        outs["p_lh"].append(lh.reshape(bp, D_LRU))
        outs["p_k"].append(_last_window(k, bp, sp))
        outs["p_v"].append(_last_window(v, bp, sp))
        outs["p_fc"].append(fc)
        xs, k, v, lc, lh, fc = _layer(
            xs, lw, _front_pad(state_lru_conv[l]), state_lru_h[l], _front_pad(state_ffn_conv[l]),
            cache_k[l].reshape(bs * WINDOW, D_KV), cache_v[l].reshape(bs * WINDOW, D_KV),
            bs, ss, s_chunk, False)
        outs["s_lc"].append(lc.transpose(1, 0, 2))
        outs["s_lh"].append(lh.reshape(bs, D_LRU))
        outs["s_k"].append(k.reshape(bs, ss, N_KV_HEADS, HEAD_DIM))
        outs["s_v"].append(v.reshape(bs, ss, N_KV_HEADS, HEAD_DIM))
        outs["s_fc"].append(fc)
    st = {n: jnp.stack(v) for n, v in outs.items()}
    return (xp.reshape(bp, sp, D_MODEL), xs.reshape(bs, ss, D_MODEL),
            st["p_lc"], st["p_lh"], st["p_k"], st["p_v"], st["p_fc"],
            st["s_lc"], st["s_lh"], st["s_k"], st["s_v"], st["s_fc"])
```

```python
import functools

import jax
import jax.numpy as jnp
from jax import lax
from jax.experimental import pallas as pl
from jax.experimental.pallas import tpu as pltpu

D_MODEL = 1024
N_HEADS = 16
N_KV_HEADS = 4
HEAD_DIM = 64
GROUP = N_HEADS // N_KV_HEADS
D_KV = N_KV_HEADS * HEAD_DIM
WINDOW = 128
D_LRU = D_MODEL
N_LRU_BLOCKS = 16
LRU_BLOCK = D_LRU // N_LRU_BLOCKS
LRU_C = 8.0
LRU_CONV_W = 4
D_FF = 3 * D_MODEL
FFN_CONV_W = 3
EPS = 1e-6
NEG = -1e30

MXU_TILE = 256
SUBLANES = 8
LANES = 128
SEG = 32
SEG_PITCH = SEG + SUBLANES
TINY = 1e-30
LOG2_E = 1.4426950408889634
PROJ_SUBTILES = 2
N_GATE_GROUPS = D_LRU // MXU_TILE

_U0, _Y0, _Q0, _G0, _K0, _V0, _D_IN = 0, 1024, 2048, 3072, 5120, 5376, 5632

VMEM_LIMIT = 56 * 1024 * 1024

BF16 = jnp.bfloat16
F32 = jnp.float32


def _dot(a, b):
    return jnp.dot(a, b, preferred_element_type=F32)


def _rms_scale(x):
    return lax.rsqrt(jnp.mean(x * x, axis=-1, keepdims=True) + EPS)


def _const_spec(shape):
    nd = len(shape)
    return pl.BlockSpec(shape, lambda *_: (0,) * nd, pipeline_mode=pl.Buffered(1))


def _head_norm(t, gain, gn):
    ms = _dot((t * t).astype(BF16), gn)
    return t * lax.rsqrt(ms + EPS) * gain


def _lru_body(u, gy, prev_u, prev_h, cs_ref, h0_ref, wc_ref, bc_ref, wai_ref, ba_ref, bi_ref, lam_ref, wbr_ref,
              us, gys, brs, *, nseg, chained, seg0=0):
    nlb = D_LRU // LANES
    taps = LRU_CONV_W - 1

    for lb in range(nlb):
        cols = slice(lb * LANES, (lb + 1) * LANES)
        for r in range(nseg):
            base = r * SEG_PITCH
            rows = slice(r * SEG, (r + 1) * SEG)
            if not chained:
                prev = cs_ref[seg0 + r, :, cols]
            elif r == 0:
                prev = prev_u[:, cols]
            else:
                prev = u[r * SEG - SUBLANES:r * SEG, cols]
            us[lb, base:base + SUBLANES, :] = prev
            us[lb, base + SUBLANES:base + SEG_PITCH, :] = u[rows, cols]
            gys[lb, base + SUBLANES:base + SEG_PITCH, :] = gy[rows, cols]

    def seg_rows(i):
        return pl.ds(SUBLANES + i, nseg, stride=SEG_PITCH)

    xc_lb, u_tail = [], []
    for lb in range(nlb):
        cols = slice(lb * LANES, (lb + 1) * LANES)
        w = [wc_ref[k, 0:nseg, cols] for k in range(LRU_CONV_W)]
        bias = bc_ref[0:nseg, cols]
        ut = [us[lb, seg_rows(i - taps), :] for i in range(SEG + taps)]
        steps = []
        for j in range(SEG):
            x = bias
            for k in range(LRU_CONV_W):
                x = x + w[k] * ut[j + k]
            steps.append(x)
        xc_lb.append(jnp.concatenate(steps, axis=0))
        u_tail.append(ut[SEG:])
    xc = jnp.concatenate(xc_lb, axis=1)

    lam = lam_ref[...]
    softplus_neg_lam = jnp.maximum(-lam, 0.0) + jnp.log1p(jnp.exp(-jnp.abs(lam)))
    log2_a_per_r = (-LRU_C * LOG2_E) * softplus_neg_lam
    xcb = xc.astype(BF16)
    a_lb, b_lb = [], []
    for j in range(N_GATE_GROUPS):
        cols = slice(j * MXU_TILE, (j + 1) * MXU_TILE)
        ri = _dot(xcb[:, cols], wai_ref[j])
        r = jax.nn.sigmoid(ri[:, :MXU_TILE] + ba_ref[:, cols])
        i = jax.nn.sigmoid(ri[:, MXU_TILE:] + bi_ref[:, cols])
        a = jnp.exp2(r * log2_a_per_r[:, cols])
        one_minus_a2 = 1.0 - a * a
        mult = one_minus_a2 * lax.rsqrt(jnp.maximum(one_minus_a2, TINY))
        b = mult * i * xc[:, cols]
        for k in range(MXU_TILE // LANES):
            a_lb.append(a[:, k * LANES:(k + 1) * LANES])
            b_lb.append(b[:, k * LANES:(k + 1) * LANES])

    hg_lb, h_last, c_next = [], [], []
    for lb in range(nlb):
        cols = slice(lb * LANES, (lb + 1) * LANES)
        step = lambda v, j: v[j * nseg:(j + 1) * nseg, :]
        if chained:
            hs, ap = jnp.zeros((nseg, LANES), F32), jnp.ones((nseg, LANES), F32)
            hs_l, ap_l = [], []
            for j in range(SEG):
                a = step(a_lb[lb], j)
                hs = a * hs + step(b_lb[lb], j)
                ap = a * ap
                hs_l.append(hs)
                ap_l.append(ap)
            c = prev_h[:, cols]
            cin = []
            for r in range(nseg):
                cin.append(c)
                c = ap[r:r + 1, :] * c + hs[r:r + 1, :]
            c_next.append(c)
            cin = jnp.concatenate(cin, axis=0)
            h_l = [hs_l[j] + ap_l[j] * cin for j in range(SEG)]
        else:
            h, h_l = h0_ref[seg0:seg0 + nseg, cols], []
            for j in range(SEG):
                h = step(a_lb[lb], j) * h + step(b_lb[lb], j)
                h_l.append(h)
            h_last.append(h)
        hg_lb.append(jnp.concatenate(
            [h_l[j] * gys[lb, seg_rows(j), :] for j in range(SEG)], axis=0))
    hg = jnp.concatenate(hg_lb, axis=1).astype(BF16)

    br = _dot(hg, wbr_ref[...])
    for lb in range(nlb):
        cols = slice(lb * LANES, (lb + 1) * LANES)
        for j in range(SEG):
            brs[lb, seg_rows(j), :] = br[j * nseg:(j + 1) * nseg, cols]
    br_nat = jnp.concatenate(
        [jnp.concatenate([brs[lb, r * SEG_PITCH + SUBLANES:(r + 1) * SEG_PITCH, :] for r in range(nseg)], axis=0)
         for lb in range(nlb)], axis=1)

    if chained:
        return br_nat, None, jnp.concatenate(c_next, axis=1)
    conv_tail = [jnp.concatenate([u_tail[lb][k] for lb in range(nlb)], axis=1) for k in range(taps)]
    return br_nat, conv_tail, jnp.concatenate(h_last, axis=1)


def _proj_lru_kernel(x_ref, gmix_ref, w_ref, bg_ref, gq_ref, gk_ref, gn_ref,
                     cs_ref, h0_ref, wc_ref, bc_ref, wai_ref, ba_ref, bi_ref, lam_ref, wbr_ref,
                     z_ref, q_ref, g1_ref, k_ref, v_ref, cso_ref, ho_ref,
                     us, gys, brs, carry_u, carry_h, *, nseg, nsub, chained):
    t = pl.program_id(1)
    tm = nseg * SEG
    taps = LRU_CONV_W - 1
    if chained:
        @pl.when(t == 0)
        def _():
            carry_u[...] = cs_ref[...]
            carry_h[...] = h0_ref[...]
        prev_u, prev_h = carry_u[...], carry_h[...]
    else:
        prev_u = prev_h = None

    xn, u, gy = [], [], []
    for s in range(nsub):
        x = x_ref[s * tm:(s + 1) * tm, :]
        xn.append((x * _rms_scale(x) * gmix_ref[...]).astype(BF16))
        u.append(_dot(xn[s], w_ref[:, _U0:_Y0]))
        gy.append(jax.nn.gelu(_dot(xn[s], w_ref[:, _Y0:_Q0])))

    for s in range(nsub):
        rows = slice(s * tm, (s + 1) * tm)
        br, conv_tail, h_end = _lru_body(
            u[s], gy[s], prev_u, prev_h, cs_ref, h0_ref, wc_ref, bc_ref, wai_ref, ba_ref, bi_ref, lam_ref,
            wbr_ref, us.at[s], gys.at[s], brs.at[s], nseg=nseg, chained=chained, seg0=s * nseg)
        gates = jax.nn.sigmoid(_dot(xn[s], w_ref[:, _G0:_K0]) + bg_ref[...])
        z_ref[rows, :] = gates[:, :D_MODEL] * br
        g1_ref[rows, :] = gates[:, D_MODEL:].astype(BF16)

        q = _dot(xn[s], w_ref[:, _Q0:_G0])
        kv = _dot(xn[s], w_ref[:, _K0:_D_IN])
        scale = HEAD_DIM ** -0.5
        for j in range(D_MODEL // MXU_TILE):
            cols = slice(j * MXU_TILE, (j + 1) * MXU_TILE)
            q_ref[rows, cols] = (_head_norm(q[:, cols], gq_ref[...], gn_ref[...]) * scale).astype(BF16)
        k_ref[rows, :] = _head_norm(kv[:, :D_KV], gk_ref[...], gn_ref[...])
        v_ref[rows, :] = kv[:, D_KV:]

        if chained:
            prev_u, prev_h = u[s][tm - SUBLANES:tm, :], h_end
        else:
            for k in range(taps):
                cso_ref[k, s * nseg:(s + 1) * nseg, :] = conv_tail[k]
            ho_ref[s * nseg:(s + 1) * nseg, :] = h_end

    if chained:
        carry_u[...] = prev_u
        carry_h[...] = prev_h

        @pl.when(t == pl.num_programs(1) - 1)
        def _():
            cso_ref[...] = prev_u[SUBLANES - taps:SUBLANES, :]
            ho_ref[...] = prev_h


def _proj_lru(x, conv_state, h_state, lw, b, t_len, chained):
    taps = LRU_CONV_W - 1
    if chained:
        nseg = SUBLANES
        nsub = PROJ_SUBTILES if t_len % (PROJ_SUBTILES * nseg * SEG) == 0 else 1
        tm = nsub * nseg * SEG
        nt = t_len // tm
        grid = (b, nt)
        row = lambda w: pl.BlockSpec((tm, w), lambda bi, ti: (bi * nt + ti, 0))
        per_b = lambda r: pl.BlockSpec((None, r, D_LRU), lambda bi, ti: (bi, 0, 0))
        state_specs = [per_b(SUBLANES), per_b(1)]
        out_state_specs = [per_b(taps), per_b(1)]
        out_state_shapes = [jax.ShapeDtypeStruct((b, taps, D_LRU), F32), jax.ShapeDtypeStruct((b, 1, D_LRU), F32)]
    else:
        assert t_len == SEG
        nseg = SUBLANES if b % SUBLANES == 0 else b
        nsub = PROJ_SUBTILES if b % (PROJ_SUBTILES * nseg) == 0 else 1
        tm, ns = nsub * nseg * SEG, nsub * nseg
        grid = (b // ns, 1)
        row = lambda w: pl.BlockSpec((tm, w), lambda bi, ti: (bi, 0))
        state_specs = [pl.BlockSpec((ns, SUBLANES, D_LRU), lambda bi, ti: (bi, 0, 0)),
                       pl.BlockSpec((ns, D_LRU), lambda bi, ti: (bi, 0))]
        out_state_specs = [pl.BlockSpec((taps, ns, D_LRU), lambda bi, ti: (0, bi, 0)),
                           pl.BlockSpec((ns, D_LRU), lambda bi, ti: (bi, 0))]
        out_state_shapes = [jax.ShapeDtypeStruct((taps, b, D_LRU), F32), jax.ShapeDtypeStruct((b, D_LRU), F32)]
    m = b * t_len
    vec = _const_spec((1, D_LRU))
    slab = pltpu.VMEM((nsub, D_LRU // LANES, nseg * SEG_PITCH, LANES), F32)
    kern = functools.partial(_proj_lru_kernel, nseg=nseg, nsub=nsub, chained=chained)
    return pl.pallas_call(
        kern,
        grid=grid,
        in_specs=[row(D_MODEL), _const_spec((1, D_MODEL)), _const_spec((D_MODEL, _D_IN)),
                  _const_spec((1, 2 * D_MODEL)), _const_spec((1, MXU_TILE)), _const_spec((1, MXU_TILE)),
                  _const_spec((MXU_TILE, MXU_TILE))] + state_specs
                 + [_const_spec((LRU_CONV_W, SUBLANES, D_LRU)), _const_spec((SUBLANES, D_LRU)),
                    _const_spec((N_GATE_GROUPS, MXU_TILE, 2 * MXU_TILE)), vec, vec, vec,
                    _const_spec((D_LRU, D_MODEL))],
        out_specs=[row(D_MODEL), row(D_MODEL), row(D_MODEL), row(D_KV), row(D_KV)] + out_state_specs,
        out_shape=[jax.ShapeDtypeStruct((m, D_MODEL), F32), jax.ShapeDtypeStruct((m, D_MODEL), BF16),
                   jax.ShapeDtypeStruct((m, D_MODEL), BF16), jax.ShapeDtypeStruct((m, D_KV), F32),
                   jax.ShapeDtypeStruct((m, D_KV), F32)] + out_state_shapes,
        scratch_shapes=[slab, slab, slab, pltpu.VMEM((SUBLANES, D_LRU), F32), pltpu.VMEM((1, D_LRU), F32)],
        compiler_params=pltpu.CompilerParams(vmem_limit_bytes=VMEM_LIMIT),
        name="proj_lru",
    )(x, lw["g_mix"], lw["w_in"], lw["b_gate"], lw["g_q"], lw["g_k"], lw["gn"], conv_state, h_state,
      lw["w_lconv"], lw["b_lconv"], lw["w_ai"], lw["b_a"], lw["b_i"], lw["lam"], lw["w_br_lru"])


def _attn_ffn_kernel(q_ref, kh_ref, vh_ref, kt_ref, vt_ref, fill_ref, z_ref, g1_ref, x_ref,
                     wba_ref, wout_ref, gffn_ref, fs_ref, wup_ref, wfc_ref, bfc_ref, wdn_ref,
                     y_ref, fso_ref, kbuf, vbuf, attn_s, gbuf, *, tq, chunk, chained):
    ti = pl.program_id(1)
    nunits = tq // chunk
    win = WINDOW + chunk
    pad = MXU_TILE - win
    col_head = lax.broadcasted_iota(jnp.int32, (1, D_KV), 1) // HEAD_DIM
    for src_h, src_t, buf in ((kh_ref, kt_ref, kbuf), (vh_ref, vt_ref, vbuf)):
        hist, rows = src_h[...].astype(BF16), src_t[...].astype(BF16)
        for kh in range(N_KV_HEADS):
            m = (col_head == kh).astype(BF16)
            if chained:
                buf[kh, 0:WINDOW, :] = hist * m
                buf[kh, WINDOW:, :] = rows * m
            else:
                for s in range(nunits):
                    buf[kh, s * win:s * win + WINDOW, :] = hist[s * WINDOW:(s + 1) * WINDOW, :] * m
                    buf[kh, s * win + WINDOW:(s + 1) * win, :] = rows[s * chunk:(s + 1) * chunk, :] * m

    zero_keys = jnp.zeros((pad, D_KV), BF16)
    key_pos = lax.broadcasted_iota(jnp.int32, (1, MXU_TILE), 1)
    for c in range(nunits):
        r0 = c * chunk
        k0 = r0 if chained else c * win
        qst = jnp.concatenate(
            [q_ref[r0:r0 + chunk, g * D_KV:(g + 1) * D_KV] for g in range(GROUP)], axis=0)

        def keys(buf):
            return jnp.concatenate(
                [blk for kh in range(N_KV_HEADS) for blk in (zero_keys, buf[kh, k0:k0 + win, :])], axis=0)

        s_all = lax.dot_general(qst, keys(kbuf), (((1,), (1,)), ((), ())), preferred_element_type=F32)
        valid = key_pos >= pad
        if chained and r0 < WINDOW:
            valid = jnp.logical_and(valid, jnp.logical_or(key_pos >= pad + WINDOW - r0, ti > 0))
        probs = []
        for kh in range(N_KV_HEADS):
            s = jnp.where(valid, s_all[:, kh * MXU_TILE:(kh + 1) * MXU_TILE], fill_ref[kh])
            e = jnp.exp(s - jnp.max(s, axis=-1, keepdims=True))
            probs.append((e * (1.0 / jnp.sum(e, axis=-1, keepdims=True))).astype(BF16))
        o = _dot(jnp.concatenate(probs, axis=1), keys(vbuf))
        for g in range(GROUP):
            attn_s[r0:r0 + chunk, g * D_KV:(g + 1) * D_KV] = o[g * chunk:(g + 1) * chunk, :].astype(BF16)

    br = _dot(attn_s[...], wba_ref[...])
    mix = (z_ref[...] + g1_ref[...].astype(F32) * br).astype(BF16)
    h = x_ref[...] + _dot(mix, wout_ref[...])
    hn = (h * _rms_scale(h) * gffn_ref[...]).astype(BF16)

    taps = FFN_CONV_W - 1
    nstreams = 1 if chained else nunits
    seg = tq // nstreams
    pitch = seg + SUBLANES
    if chained:
        @pl.when(ti == 0)
        def _():
            gbuf[0:SUBLANES, :] = fs_ref[...]
    gate = _dot(hn, wup_ref[:, :D_FF])
    up = _dot(hn, wup_ref[:, D_FF:])
    parts = []
    for r in range(nstreams):
        base = r * pitch
        if not chained:
            gbuf[base:base + SUBLANES, :] = fs_ref[r]
        gbuf[base + SUBLANES:base + pitch, :] = gate[r * seg:(r + 1) * seg, :]
        gc = bfc_ref[...]
        for j in range(FFN_CONV_W):
            lo = base + SUBLANES - taps + j
            gc = gc + wfc_ref[j:j + 1, :] * gbuf[lo:lo + seg, :]
        parts.append(gc)
    gc = parts[0] if nstreams == 1 else jnp.concatenate(parts, axis=0)
    act = (jax.nn.gelu(gc) * up).astype(BF16)
    y_ref[...] = h + _dot(act, wdn_ref[...])

    if chained:
        gbuf[0:SUBLANES, :] = gbuf[tq:tq + SUBLANES, :]

        @pl.when(ti == pl.num_programs(1) - 1)
        def _():
            fso_ref[...] = gbuf[SUBLANES - taps:SUBLANES, :]
    else:
        for r in range(nstreams):
            fso_ref[r] = gbuf[(r + 1) * pitch - taps:(r + 1) * pitch, :]


def _attn_ffn(q, k, v, k_hist, v_hist, z, g1, x, ffn_state, lw, b, t_len, chunk, chained):
    taps = FFN_CONV_W - 1
    if chained:
        tq, ns = _pick(t_len, 256), 1
        nt = t_len // tq
        grid = (b, nt)
        row = lambda w: pl.BlockSpec((tq, w), lambda bi, ti: (bi * nt + ti, 0))
        per_b, per_t = t_len // WINDOW, tq // WINDOW
        hist = pl.BlockSpec((WINDOW, D_KV), lambda bi, ti: (jnp.maximum(bi * per_b + ti * per_t - 1, 0), 0))
        kv_rows = WINDOW + tq
        state = lambda r: pl.BlockSpec((None, r, D_FF), lambda bi, ti: (bi, 0, 0))
    else:
        assert t_len == chunk
        ns = SUBLANES if b % SUBLANES == 0 else b
        tq = ns * chunk
        grid = (b // ns, 1)
        row = lambda w: pl.BlockSpec((tq, w), lambda bi, ti: (bi, 0))
        hist = pl.BlockSpec((ns * WINDOW, D_KV), lambda bi, ti: (bi, 0))
        kv_rows = ns * (WINDOW + chunk)
        state = lambda r: pl.BlockSpec((ns, r, D_FF), lambda bi, ti: (bi, 0, 0))
    kern = functools.partial(_attn_ffn_kernel, tq=tq, chunk=chunk, chained=chained)
    return pl.pallas_call(
        kern,
        grid=grid,
        in_specs=[row(D_MODEL), hist, hist, row(D_KV), row(D_KV),
                  _const_spec((N_KV_HEADS, GROUP * chunk, MXU_TILE)),
                  row(D_MODEL), row(D_MODEL), row(D_MODEL),
                  _const_spec((D_MODEL, D_MODEL)), _const_spec((D_MODEL, D_MODEL)), _const_spec((1, D_MODEL)),
                  state(SUBLANES), _const_spec((D_MODEL, 2 * D_FF)),
                  _const_spec((FFN_CONV_W, D_FF)), _const_spec((1, D_FF)), _const_spec((D_FF, D_MODEL))],
        out_specs=[row(D_MODEL), state(taps)],
        out_shape=[jax.ShapeDtypeStruct((b * t_len, D_MODEL), F32),
                   jax.ShapeDtypeStruct((b, taps, D_FF), F32)],
        scratch_shapes=[pltpu.VMEM((N_KV_HEADS, kv_rows, D_KV), BF16)] * 2
                       + [pltpu.VMEM((tq, D_MODEL), BF16),
                          pltpu.VMEM((ns * (tq // ns + SUBLANES), D_FF), F32)],
        compiler_params=pltpu.CompilerParams(vmem_limit_bytes=VMEM_LIMIT),
        name="attn_ffn",
    )(q, k_hist, v_hist, k, v, lw["sink_fill"][chunk], z, g1, x, lw["w_br_attn"], lw["w_out"], lw["g_ffn"],
      ffn_state, lw["w_up"], lw["w_fconv"], lw["b_fconv"], lw["w_down"])


def _block_diag_groups(w):
    per = MXU_TILE // LRU_BLOCK
    w4 = w.reshape(N_GATE_GROUPS, per, LRU_BLOCK, LRU_BLOCK)
    return jnp.einsum("jncd,nm->jncmd", w4, jnp.eye(per, dtype=w.dtype)).reshape(
        N_GATE_GROUPS, MXU_TILE, MXU_TILE)


def _prep_layer(l, p, chunks):
    w_in = p["w_in"][l]
    wq = w_in[:, 2048:3072].reshape(D_MODEL, N_KV_HEADS, GROUP, HEAD_DIM).transpose(0, 2, 1, 3)
    w_in_p = jnp.concatenate(
        [w_in[:, :2048], wq.reshape(D_MODEL, D_MODEL), w_in[:, 3584:], w_in[:, 3072:3584]], axis=1)
    w_ba = p["w_br_attn"][l].reshape(N_KV_HEADS, GROUP, HEAD_DIM, D_MODEL).transpose(1, 0, 2, 3)
    per = MXU_TILE // HEAD_DIM
    sink = p["attn_sink"][l].reshape(N_KV_HEADS, GROUP)
    row = lambda v: v.reshape(1, -1)
    return {
        "g_mix": row(p["g_mix"][l]),
        "w_in": w_in_p.astype(BF16),
        "b_gate": row(p["b_gate"][l]),
        "g_q": row(jnp.tile(p["g_q"][l], per)),
        "g_k": row(jnp.tile(p["g_k"][l], per)),
        "gn": (jnp.kron(jnp.eye(per, dtype=F32), jnp.ones((HEAD_DIM, HEAD_DIM), F32)) / HEAD_DIM).astype(BF16),
        "w_lconv": jnp.broadcast_to(p["w_lru_conv"][l][:, None, :], (LRU_CONV_W, SUBLANES, D_LRU)),
        "b_lconv": jnp.broadcast_to(p["b_lru_conv"][l][None, :], (SUBLANES, D_LRU)),
        "w_ai": jnp.concatenate([_block_diag_groups(p["w_lru_a"][l]), _block_diag_groups(p["w_lru_i"][l])],
                                axis=2).astype(BF16),
        "b_a": row(p["b_lru_a"][l]),
        "b_i": row(p["b_lru_i"][l]),
        "lam": row(p["lru_lambda"][l]),
        "w_br_lru": p["w_br_lru"][l].astype(BF16),
        "sink_fill": {c: jnp.full((N_KV_HEADS, GROUP * c, MXU_TILE), NEG, F32).at[:, :, 0].set(
            jnp.repeat(sink, c, axis=1)) for c in chunks},
        "w_br_attn": w_ba.reshape(D_MODEL, D_MODEL).astype(BF16),
        "w_out": p["w_out"][l].astype(BF16),
        "g_ffn": row(p["g_ffn"][l]),
        "w_up": p["w_up"][l].astype(BF16),
        "w_fconv": p["w_ffn_conv"][l],
        "b_fconv": row(p["b_ffn_conv"][l]),
        "w_down": p["w_down"][l].astype(BF16),
    }


def _pick(n, pref):
    return pref if n % pref == 0 else n


def _layer(x, lw, conv_state, h_state, ffn_state, k_hist, v_hist, b, t_len, chunk, chained):
    z, q, g1, k, v, conv_new, h_new = _proj_lru(x, conv_state, h_state, lw, b, t_len, chained)
    if chained:
        k_hist, v_hist = k, v
    y, ffn_new = _attn_ffn(q, k, v, k_hist, v_hist, z, g1, x, ffn_state, lw, b, t_len, chunk, chained)
    return y, k, v, conv_new, h_new, ffn_new


def _front_pad(state):
    return jnp.pad(state, ((0, 0), (SUBLANES - state.shape[1], 0), (0, 0)))


def kernel(x_prompt, x_sample, state_lru_conv, state_lru_h, cache_k, cache_v, state_ffn_conv, g_mix, w_in, b_gate, w_lru_conv, b_lru_conv, w_lru_a, b_lru_a, w_lru_i, b_lru_i, lru_lambda, g_q, g_k, attn_sink, w_br_lru, w_br_attn, w_out, g_ffn, w_up, w_ffn_conv, b_ffn_conv, w_down):
    params = dict(g_mix=g_mix, w_in=w_in, b_gate=b_gate, w_lru_conv=w_lru_conv, b_lru_conv=b_lru_conv,
                  w_lru_a=w_lru_a, b_lru_a=b_lru_a, w_lru_i=w_lru_i, b_lru_i=b_lru_i, lru_lambda=lru_lambda,
                  g_q=g_q, g_k=g_k, attn_sink=attn_sink, w_br_lru=w_br_lru, w_br_attn=w_br_attn, w_out=w_out,
                  g_ffn=g_ffn, w_up=w_up, w_ffn_conv=w_ffn_conv, b_ffn_conv=b_ffn_conv, w_down=w_down)
    depth = w_in.shape[0]
    bp, sp, _ = x_prompt.shape
    bs, ss, _ = x_sample.shape
    p_chunk, s_chunk = 64, ss
    xp = x_prompt.reshape(bp * sp, D_MODEL)
    xs = x_sample.reshape(bs * ss, D_MODEL)
    zeros_p = (jnp.zeros((bp, SUBLANES, D_LRU), F32), jnp.zeros((bp, 1, D_LRU), F32),
               jnp.zeros((bp, SUBLANES, D_FF), F32))
    ck = cache_k.reshape(depth, bs * WINDOW, D_KV)
    cv = cache_v.reshape(depth, bs * WINDOW, D_KV)
    outs = {n: [] for n in ("p_lc", "p_lh", "p_k", "p_v", "p_fc", "s_lc", "s_lh", "s_k", "s_v", "s_fc")}
    for l in range(depth):
        lw = _prep_layer(l, params, {p_chunk, s_chunk})
        xp, k, v, lc, lh, fc = _layer(xp, lw, *zeros_p, None, None, bp, sp, p_chunk, True)
        outs["p_lc"].append(lc)
        outs["p_lh"].append(lh.reshape(bp, D_LRU))
        outs["p_k"].append(k.reshape(bp, sp, D_KV)[:, sp - WINDOW:])
        outs["p_v"].append(v.reshape(bp, sp, D_KV)[:, sp - WINDOW:])
        outs["p_fc"].append(fc)
        xs, k, v, lc, lh, fc = _layer(
            xs, lw, _front_pad(state_lru_conv[l]), state_lru_h[l], _front_pad(state_ffn_conv[l]),
            ck[l], cv[l],
            bs, ss, s_chunk, False)
        outs["s_lc"].append(lc.transpose(1, 0, 2))
        outs["s_lh"].append(lh.reshape(bs, D_LRU))
        outs["s_k"].append(k)
        outs["s_v"].append(v)
        outs["s_fc"].append(fc)
    st = {n: jnp.stack(v) for n, v in outs.items()}
    heads = lambda a, b, t: a.reshape(depth, b, t, N_KV_HEADS, HEAD_DIM)
    return (xp.reshape(bp, sp, D_MODEL), xs.reshape(bs, ss, D_MODEL),
            st["p_lc"], st["p_lh"], heads(st["p_k"], bp, WINDOW), heads(st["p_v"], bp, WINDOW), st["p_fc"],
            st["s_lc"], st["s_lh"], heads(st["s_k"], bs, ss), heads(st["s_v"], bs, ss), st["s_fc"])
```

```python
import functools

import jax
import jax.numpy as jnp
from jax import lax
from jax.experimental import pallas as pl
from jax.experimental.pallas import tpu as pltpu

D_MODEL = 1024
N_HEADS = 16
N_KV_HEADS = 4
HEAD_DIM = 64
GROUP = N_HEADS // N_KV_HEADS
D_KV = N_KV_HEADS * HEAD_DIM
WINDOW = 128
D_LRU = D_MODEL
N_LRU_BLOCKS = 16
LRU_BLOCK = D_LRU // N_LRU_BLOCKS
LRU_C = 8.0
LRU_CONV_W = 4
D_FF = 3 * D_MODEL
FFN_CONV_W = 3
EPS = 1e-6
NEG = -1e30

MXU_TILE = 256
SUBLANES = 8
LANES = 128
SEG = 32
SEG_PITCH = SEG + SUBLANES
TINY = 1e-30
LOG2_E = 1.4426950408889634
PROJ_CHUNK = 512
PROJ_TICKS_PER_CHUNK = 2
N_GATE_GROUPS = D_LRU // MXU_TILE

_U0, _Y0, _Q0, _G0, _K0, _V0, _D_IN = 0, 1024, 2048, 3072, 5120, 5376, 5632

VMEM_LIMIT = 56 * 1024 * 1024

BF16 = jnp.bfloat16
F32 = jnp.float32


def _dot(a, b):
    return jnp.dot(a, b, preferred_element_type=F32)


def _rms_scale(x):
    return lax.rsqrt(jnp.mean(x * x, axis=-1, keepdims=True) + EPS)


def _const_spec(shape):
    nd = len(shape)
    return pl.BlockSpec(shape, lambda *_: (0,) * nd, pipeline_mode=pl.Buffered(1))


def _head_norm(t, gain, gn):
    ms = _dot((t * t).astype(BF16), gn)
    return t * lax.rsqrt(ms + EPS) * gain


def _lru_body(u, gy, cs_ref, h0_ref, wc_ref, bc_ref, wai_ref, ba_ref, bi_ref, lam_ref, wbr_ref,
              us, gys, brs, carry_u, carry_h, tick, *, nseg, chained):
    nlb = D_LRU // LANES
    taps = LRU_CONV_W - 1

    for lb in range(nlb):
        cols = slice(lb * LANES, (lb + 1) * LANES)
        for r in range(nseg):
            base = r * SEG_PITCH
            rows = slice(r * SEG, (r + 1) * SEG)
            if not chained:
                prev = cs_ref[r, :, cols]
            elif r == 0:
                prev = carry_u[:, cols]
            else:
                prev = u[r * SEG - SUBLANES:r * SEG, cols]
            us[lb, base:base + SUBLANES, :] = prev
            us[lb, base + SUBLANES:base + SEG_PITCH, :] = u[rows, cols]
            gys[lb, base + SUBLANES:base + SEG_PITCH, :] = gy[rows, cols]

    def seg_rows(i):
        return pl.ds(SUBLANES + i, nseg, stride=SEG_PITCH)

    xc_lb, u_tail = [], []
    for lb in range(nlb):
        cols = slice(lb * LANES, (lb + 1) * LANES)
        w = [wc_ref[k, 0:nseg, cols] for k in range(LRU_CONV_W)]
        bias = bc_ref[0:nseg, cols]
        ut = [us[lb, seg_rows(i - taps), :] for i in range(SEG + taps)]
        steps = []
        for j in range(SEG):
            x = bias
            for k in range(LRU_CONV_W):
                x = x + w[k] * ut[j + k]
            steps.append(x)
        xc_lb.append(jnp.concatenate(steps, axis=0))
        u_tail.append(ut[SEG:])
        tick()
    xc = jnp.concatenate(xc_lb, axis=1)

    lam = lam_ref[...]
    softplus_neg_lam = jnp.maximum(-lam, 0.0) + jnp.log1p(jnp.exp(-jnp.abs(lam)))
    log2_a_per_r = (-LRU_C * LOG2_E) * softplus_neg_lam
    xcb = xc.astype(BF16)
    a_lb, b_lb = [], []
    for j in range(N_GATE_GROUPS):
        cols = slice(j * MXU_TILE, (j + 1) * MXU_TILE)
        ri = _dot(xcb[:, cols], wai_ref[j])
        r = jax.nn.sigmoid(ri[:, :MXU_TILE] + ba_ref[:, cols])
        i = jax.nn.sigmoid(ri[:, MXU_TILE:] + bi_ref[:, cols])
        a = jnp.exp2(r * log2_a_per_r[:, cols])
        one_minus_a2 = 1.0 - a * a
        mult = one_minus_a2 * lax.rsqrt(jnp.maximum(one_minus_a2, TINY))
        b = mult * i * xc[:, cols]
        for k in range(MXU_TILE // LANES):
            a_lb.append(a[:, k * LANES:(k + 1) * LANES])
            b_lb.append(b[:, k * LANES:(k + 1) * LANES])
        tick()

    hg_lb, h_last, c_next = [], [], []
    for lb in range(nlb):
        cols = slice(lb * LANES, (lb + 1) * LANES)
        step = lambda v, j: v[j * nseg:(j + 1) * nseg, :]
        if chained:
            hs, ap = jnp.zeros((nseg, LANES), F32), jnp.ones((nseg, LANES), F32)
            hs_l, ap_l = [], []
            for j in range(SEG):
                a = step(a_lb[lb], j)
                hs = a * hs + step(b_lb[lb], j)
                ap = a * ap
                hs_l.append(hs)
                ap_l.append(ap)
            c = carry_h[:, cols]
            cin = []
            for r in range(nseg):
                cin.append(c)
                c = ap[r:r + 1, :] * c + hs[r:r + 1, :]
            c_next.append(c)
            cin = jnp.concatenate(cin, axis=0)
            h_l = [hs_l[j] + ap_l[j] * cin for j in range(SEG)]
        else:
            h, h_l = h0_ref[:, cols], []
            for j in range(SEG):
                h = step(a_lb[lb], j) * h + step(b_lb[lb], j)
                h_l.append(h)
            h_last.append(h)
        hg_lb.append(jnp.concatenate(
            [h_l[j] * gys[lb, seg_rows(j), :] for j in range(SEG)], axis=0))
        tick()
    hg = jnp.concatenate(hg_lb, axis=1).astype(BF16)

    br = _dot(hg, wbr_ref[...])
    for lb in range(nlb):
        cols = slice(lb * LANES, (lb + 1) * LANES)
        for j in range(SEG):
            brs[lb, seg_rows(j), :] = br[j * nseg:(j + 1) * nseg, cols]
    br_nat = jnp.concatenate(
        [jnp.concatenate([brs[lb, r * SEG_PITCH + SUBLANES:(r + 1) * SEG_PITCH, :] for r in range(nseg)], axis=0)
         for lb in range(nlb)], axis=1)

    if chained:
        return br_nat, None, jnp.concatenate(c_next, axis=1)
    conv_tail = [jnp.concatenate([u_tail[lb][k] for lb in range(nlb)], axis=1) for k in range(taps)]
    return br_nat, conv_tail, jnp.concatenate(h_last, axis=1)


def _proj_lru_kernel(x_ref, gmix_ref, w_ref, bg_ref, gq_ref, gk_ref, gn_ref,
                     cs_ref, h0_ref, wc_ref, bc_ref, wai_ref, ba_ref, bi_ref, lam_ref, wbr_ref,
                     z_ref, q_ref, g1_ref, k_ref, v_ref, cso_ref, ho_ref,
                     us, gys, brs, carry_u, carry_h, *, nseg, chained):
    t = pl.program_id(1)
    tm = nseg * SEG
    taps = LRU_CONV_W - 1
    if chained:
        @pl.when(t == 0)
        def _():
            carry_u[...] = cs_ref[...]
            carry_h[...] = h0_ref[...]

    x = x_ref[...]
    xn = (x * _rms_scale(x) * gmix_ref[...]).astype(BF16)

    def proj(lo, hi):
        return _dot(xn, w_ref[:, lo:hi])

    u = proj(_U0, _Y0)
    gy = jax.nn.gelu(proj(_Y0, _Q0))

    scale = HEAD_DIM ** -0.5
    g0_parts = []

    def gate_chunk(j):
        lo = _G0 + j * PROJ_CHUNK
        g = jax.nn.sigmoid(proj(lo, lo + PROJ_CHUNK) + bg_ref[:, j * PROJ_CHUNK:(j + 1) * PROJ_CHUNK])
        if lo + PROJ_CHUNK <= _G0 + D_MODEL:
            g0_parts.append(g)
        else:
            g1_ref[:, lo - _G0 - D_MODEL:lo - _G0 - D_MODEL + PROJ_CHUNK] = g.astype(BF16)

    def q_chunk(j):
        q = proj(_Q0 + j * PROJ_CHUNK, _Q0 + (j + 1) * PROJ_CHUNK)
        for i in range(PROJ_CHUNK // MXU_TILE):
            cols = slice(j * PROJ_CHUNK + i * MXU_TILE, j * PROJ_CHUNK + (i + 1) * MXU_TILE)
            blk = q[:, i * MXU_TILE:(i + 1) * MXU_TILE]
            q_ref[:, cols] = (_head_norm(blk, gq_ref[...], gn_ref[...]) * scale).astype(BF16)

    def kv_chunk():
        kv = proj(_K0, _D_IN)
        k_ref[...] = _head_norm(kv[:, :D_KV], gk_ref[...], gn_ref[...])
        v_ref[...] = kv[:, D_KV:]

    pending = ([functools.partial(gate_chunk, j) for j in range(2 * D_MODEL // PROJ_CHUNK)]
               + [functools.partial(q_chunk, j) for j in range(D_MODEL // PROJ_CHUNK)] + [kv_chunk])
    ticks = [0]

    def tick():
        ticks[0] += 1
        if pending and ticks[0] % PROJ_TICKS_PER_CHUNK == 0:
            pending.pop(0)()

    br, conv_tail, h_end = _lru_body(
        u, gy, cs_ref, h0_ref, wc_ref, bc_ref, wai_ref, ba_ref, bi_ref, lam_ref, wbr_ref,
        us, gys, brs, carry_u, carry_h, tick, nseg=nseg, chained=chained)
    while pending:
        pending.pop(0)()
    z_ref[...] = jnp.concatenate(g0_parts, axis=1) * br

    if chained:
        carry_u[...] = u[tm - SUBLANES:tm, :]
        carry_h[...] = h_end

        @pl.when(t == pl.num_programs(1) - 1)
        def _():
            cso_ref[...] = u[tm - taps:tm, :]
            ho_ref[...] = h_end
    else:
        for k in range(taps):
            cso_ref[k] = conv_tail[k]
        ho_ref[...] = h_end


def _proj_lru(x, conv_state, h_state, lw, b, t_len, chained):
    taps = LRU_CONV_W - 1
    if chained:
        nseg = SUBLANES
        tm = nseg * SEG
        nt = t_len // tm
        grid = (b, nt)
        row = lambda w: pl.BlockSpec((tm, w), lambda bi, ti: (bi * nt + ti, 0))
        per_b = lambda r: pl.BlockSpec((None, r, D_LRU), lambda bi, ti: (bi, 0, 0))
        state_specs = [per_b(SUBLANES), per_b(1)]
        out_state_specs = [per_b(taps), per_b(1)]
        out_state_shapes = [jax.ShapeDtypeStruct((b, taps, D_LRU), F32), jax.ShapeDtypeStruct((b, 1, D_LRU), F32)]
    else:
        assert t_len == SEG
        nseg = SUBLANES if b % SUBLANES == 0 else b
        tm = nseg * SEG
        grid = (b // nseg, 1)
        row = lambda w: pl.BlockSpec((tm, w), lambda bi, ti: (bi, 0))
        state_specs = [pl.BlockSpec((nseg, SUBLANES, D_LRU), lambda bi, ti: (bi, 0, 0)),
                       pl.BlockSpec((nseg, D_LRU), lambda bi, ti: (bi, 0))]
        out_state_specs = [pl.BlockSpec((taps, nseg, D_LRU), lambda bi, ti: (0, bi, 0)),
                           pl.BlockSpec((nseg, D_LRU), lambda bi, ti: (bi, 0))]
        out_state_shapes = [jax.ShapeDtypeStruct((taps, b, D_LRU), F32), jax.ShapeDtypeStruct((b, D_LRU), F32)]
    m = b * t_len
    vec = _const_spec((1, D_LRU))
    slab = pltpu.VMEM((D_LRU // LANES, nseg * SEG_PITCH, LANES), F32)
    kern = functools.partial(_proj_lru_kernel, nseg=nseg, chained=chained)
    return pl.pallas_call(
        kern,
        grid=grid,
        in_specs=[row(D_MODEL), _const_spec((1, D_MODEL)), _const_spec((D_MODEL, _D_IN)),
                  _const_spec((1, 2 * D_MODEL)), _const_spec((1, MXU_TILE)), _const_spec((1, MXU_TILE)),
                  _const_spec((MXU_TILE, MXU_TILE))] + state_specs
                 + [_const_spec((LRU_CONV_W, SUBLANES, D_LRU)), _const_spec((SUBLANES, D_LRU)),
                    _const_spec((N_GATE_GROUPS, MXU_TILE, 2 * MXU_TILE)), vec, vec, vec,
                    _const_spec((D_LRU, D_MODEL))],
        out_specs=[row(D_MODEL), row(D_MODEL), row(D_MODEL), row(D_KV), row(D_KV)] + out_state_specs,
        out_shape=[jax.ShapeDtypeStruct((m, D_MODEL), F32), jax.ShapeDtypeStruct((m, D_MODEL), BF16),
                   jax.ShapeDtypeStruct((m, D_MODEL), BF16), jax.ShapeDtypeStruct((m, D_KV), F32),
                   jax.ShapeDtypeStruct((m, D_KV), F32)] + out_state_shapes,
        scratch_shapes=[slab, slab, slab, pltpu.VMEM((SUBLANES, D_LRU), F32), pltpu.VMEM((1, D_LRU), F32)],
        compiler_params=pltpu.CompilerParams(vmem_limit_bytes=VMEM_LIMIT),
        name="proj_lru",
    )(x, lw["g_mix"], lw["w_in"], lw["b_gate"], lw["g_q"], lw["g_k"], lw["gn"], conv_state, h_state,
      lw["w_lconv"], lw["b_lconv"], lw["w_ai"], lw["b_a"], lw["b_i"], lw["lam"], lw["w_br_lru"])


def _attn_ffn_kernel(q_ref, kh_ref, vh_ref, kt_ref, vt_ref, fill_ref, z_ref, g1_ref, x_ref,
                     wba_ref, wout_ref, gffn_ref, fs_ref, wup_ref, wfc_ref, bfc_ref, wdn_ref,
                     y_ref, fso_ref, kbuf, vbuf, attn_s, gbuf, *, tq, chunk, chained):
    ti = pl.program_id(1)
    nunits = tq // chunk
    win = WINDOW + chunk
    pad = MXU_TILE - win
    col_head = lax.broadcasted_iota(jnp.int32, (1, D_KV), 1) // HEAD_DIM
    for src_h, src_t, buf in ((kh_ref, kt_ref, kbuf), (vh_ref, vt_ref, vbuf)):
        hist, rows = src_h[...].astype(BF16), src_t[...].astype(BF16)
        for kh in range(N_KV_HEADS):
            m = (col_head == kh).astype(BF16)
            if chained:
                buf[kh, 0:WINDOW, :] = hist * m
                buf[kh, WINDOW:, :] = rows * m
            else:
                for s in range(nunits):
                    buf[kh, s * win:s * win + WINDOW, :] = hist[s * WINDOW:(s + 1) * WINDOW, :] * m
                    buf[kh, s * win + WINDOW:(s + 1) * win, :] = rows[s * chunk:(s + 1) * chunk, :] * m

    zero_keys = jnp.zeros((pad, D_KV), BF16)
    key_pos = lax.broadcasted_iota(jnp.int32, (1, MXU_TILE), 1)
    for c in range(nunits):
        r0 = c * chunk
        k0 = r0 if chained else c * win
        qst = jnp.concatenate(
            [q_ref[r0:r0 + chunk, g * D_KV:(g + 1) * D_KV] for g in range(GROUP)], axis=0)

        def keys(buf):
            return jnp.concatenate(
                [blk for kh in range(N_KV_HEADS) for blk in (zero_keys, buf[kh, k0:k0 + win, :])], axis=0)

        s_all = lax.dot_general(qst, keys(kbuf), (((1,), (1,)), ((), ())), preferred_element_type=F32)
        valid = key_pos >= pad
        if chained and r0 < WINDOW:
            valid = jnp.logical_and(valid, jnp.logical_or(key_pos >= pad + WINDOW - r0, ti > 0))
        probs = []
        for kh in range(N_KV_HEADS):
            s = jnp.where(valid, s_all[:, kh * MXU_TILE:(kh + 1) * MXU_TILE], fill_ref[kh])
            e = jnp.exp(s - jnp.max(s, axis=-1, keepdims=True))
            probs.append((e * (1.0 / jnp.sum(e, axis=-1, keepdims=True))).astype(BF16))
        o = _dot(jnp.concatenate(probs, axis=1), keys(vbuf))
        for g in range(GROUP):
            attn_s[r0:r0 + chunk, g * D_KV:(g + 1) * D_KV] = o[g * chunk:(g + 1) * chunk, :].astype(BF16)

    br = _dot(attn_s[...], wba_ref[...])
    mix = (z_ref[...] + g1_ref[...].astype(F32) * br).astype(BF16)
    h = x_ref[...] + _dot(mix, wout_ref[...])
    hn = (h * _rms_scale(h) * gffn_ref[...]).astype(BF16)

    taps = FFN_CONV_W - 1
    nstreams = 1 if chained else nunits
    seg = tq // nstreams
    pitch = seg + SUBLANES
    if chained:
        @pl.when(ti == 0)
        def _():
            gbuf[0:SUBLANES, :] = fs_ref[...]
    gate = _dot(hn, wup_ref[:, :D_FF])
    up = _dot(hn, wup_ref[:, D_FF:])
    parts = []
    for r in range(nstreams):
        base = r * pitch
        if not chained:
            gbuf[base:base + SUBLANES, :] = fs_ref[r]
        gbuf[base + SUBLANES:base + pitch, :] = gate[r * seg:(r + 1) * seg, :]
        gc = bfc_ref[...]
        for j in range(FFN_CONV_W):
            lo = base + SUBLANES - taps + j
            gc = gc + wfc_ref[j:j + 1, :] * gbuf[lo:lo + seg, :]
        parts.append(gc)
    gc = parts[0] if nstreams == 1 else jnp.concatenate(parts, axis=0)
    act = (jax.nn.gelu(gc) * up).astype(BF16)
    y_ref[...] = h + _dot(act, wdn_ref[...])

    if chained:
        gbuf[0:SUBLANES, :] = gbuf[tq:tq + SUBLANES, :]

        @pl.when(ti == pl.num_programs(1) - 1)
        def _():
            fso_ref[...] = gbuf[SUBLANES - taps:SUBLANES, :]
    else:
        for r in range(nstreams):
            fso_ref[r] = gbuf[(r + 1) * pitch - taps:(r + 1) * pitch, :]


def _attn_ffn(q, k, v, k_hist, v_hist, z, g1, x, ffn_state, lw, b, t_len, chunk, chained):
    taps = FFN_CONV_W - 1
    if chained:
        tq, ns = _pick(t_len, 256), 1
        nt = t_len // tq
        grid = (b, nt)
        row = lambda w: pl.BlockSpec((tq, w), lambda bi, ti: (bi * nt + ti, 0))
        per_b, per_t = t_len // WINDOW, tq // WINDOW
        hist = pl.BlockSpec((WINDOW, D_KV), lambda bi, ti: (jnp.maximum(bi * per_b + ti * per_t - 1, 0), 0))
        kv_rows = WINDOW + tq
        state = lambda r: pl.BlockSpec((None, r, D_FF), lambda bi, ti: (bi, 0, 0))
    else:
        assert t_len == chunk
        ns = SUBLANES if b % SUBLANES == 0 else b
        tq = ns * chunk
        grid = (b // ns, 1)
        row = lambda w: pl.BlockSpec((tq, w), lambda bi, ti: (bi, 0))
        hist = pl.BlockSpec((ns * WINDOW, D_KV), lambda bi, ti: (bi, 0))
        kv_rows = ns * (WINDOW + chunk)
        state = lambda r: pl.BlockSpec((ns, r, D_FF), lambda bi, ti: (bi, 0, 0))
    kern = functools.partial(_attn_ffn_kernel, tq=tq, chunk=chunk, chained=chained)
    return pl.pallas_call(
        kern,
        grid=grid,
        in_specs=[row(D_MODEL), hist, hist, row(D_KV), row(D_KV),
                  _const_spec((N_KV_HEADS, GROUP * chunk, MXU_TILE)),
                  row(D_MODEL), row(D_MODEL), row(D_MODEL),
                  _const_spec((D_MODEL, D_MODEL)), _const_spec((D_MODEL, D_MODEL)), _const_spec((1, D_MODEL)),
                  state(SUBLANES), _const_spec((D_MODEL, 2 * D_FF)),
                  _const_spec((FFN_CONV_W, D_FF)), _const_spec((1, D_FF)), _const_spec((D_FF, D_MODEL))],
        out_specs=[row(D_MODEL), state(taps)],
        out_shape=[jax.ShapeDtypeStruct((b * t_len, D_MODEL), F32),
                   jax.ShapeDtypeStruct((b, taps, D_FF), F32)],
        scratch_shapes=[pltpu.VMEM((N_KV_HEADS, kv_rows, D_KV), BF16)] * 2
                       + [pltpu.VMEM((tq, D_MODEL), BF16),
                          pltpu.VMEM((ns * (tq // ns + SUBLANES), D_FF), F32)],
        compiler_params=pltpu.CompilerParams(vmem_limit_bytes=VMEM_LIMIT),
        name="attn_ffn",
    )(q, k_hist, v_hist, k, v, lw["sink_fill"][chunk], z, g1, x, lw["w_br_attn"], lw["w_out"], lw["g_ffn"],
      ffn_state, lw["w_up"], lw["w_fconv"], lw["b_fconv"], lw["w_down"])


def _block_diag_groups(w):
    per = MXU_TILE // LRU_BLOCK
    w4 = w.reshape(N_GATE_GROUPS, per, LRU_BLOCK, LRU_BLOCK)
    return jnp.einsum("jncd,nm->jncmd", w4, jnp.eye(per, dtype=w.dtype)).reshape(
        N_GATE_GROUPS, MXU_TILE, MXU_TILE)


def _prep_layer(l, p, chunks):
    w_in = p["w_in"][l]
    wq = w_in[:, 2048:3072].reshape(D_MODEL, N_KV_HEADS, GROUP, HEAD_DIM).transpose(0, 2, 1, 3)
    w_in_p = jnp.concatenate(
        [w_in[:, :2048], wq.reshape(D_MODEL, D_MODEL), w_in[:, 3584:], w_in[:, 3072:3584]], axis=1)
    w_ba = p["w_br_attn"][l].reshape(N_KV_HEADS, GROUP, HEAD_DIM, D_MODEL).transpose(1, 0, 2, 3)
    per = MXU_TILE // HEAD_DIM
    sink = p["attn_sink"][l].reshape(N_KV_HEADS, GROUP)
    row = lambda v: v.reshape(1, -1)
    return {
        "g_mix": row(p["g_mix"][l]),
        "w_in": w_in_p.astype(BF16),
        "b_gate": row(p["b_gate"][l]),
        "g_q": row(jnp.tile(p["g_q"][l], per)),
        "g_k": row(jnp.tile(p["g_k"][l], per)),
        "gn": (jnp.kron(jnp.eye(per, dtype=F32), jnp.ones((HEAD_DIM, HEAD_DIM), F32)) / HEAD_DIM).astype(BF16),
        "w_lconv": jnp.broadcast_to(p["w_lru_conv"][l][:, None, :], (LRU_CONV_W, SUBLANES, D_LRU)),
        "b_lconv": jnp.broadcast_to(p["b_lru_conv"][l][None, :], (SUBLANES, D_LRU)),
        "w_ai": jnp.concatenate([_block_diag_groups(p["w_lru_a"][l]), _block_diag_groups(p["w_lru_i"][l])],
                                axis=2).astype(BF16),
        "b_a": row(p["b_lru_a"][l]),
        "b_i": row(p["b_lru_i"][l]),
        "lam": row(p["lru_lambda"][l]),
        "w_br_lru": p["w_br_lru"][l].astype(BF16),
        "sink_fill": {c: jnp.full((N_KV_HEADS, GROUP * c, MXU_TILE), NEG, F32).at[:, :, 0].set(
            jnp.repeat(sink, c, axis=1)) for c in chunks},
        "w_br_attn": w_ba.reshape(D_MODEL, D_MODEL).astype(BF16),
        "w_out": p["w_out"][l].astype(BF16),
        "g_ffn": row(p["g_ffn"][l]),
        "w_up": p["w_up"][l].astype(BF16),
        "w_fconv": p["w_ffn_conv"][l],
        "b_fconv": row(p["b_ffn_conv"][l]),
        "w_down": p["w_down"][l].astype(BF16),
    }


def _pick(n, pref):
    return pref if n % pref == 0 else n


def _layer(x, lw, conv_state, h_state, ffn_state, k_hist, v_hist, b, t_len, chunk, chained):
    z, q, g1, k, v, conv_new, h_new = _proj_lru(x, conv_state, h_state, lw, b, t_len, chained)
    if chained:
        k_hist, v_hist = k, v
    y, ffn_new = _attn_ffn(q, k, v, k_hist, v_hist, z, g1, x, ffn_state, lw, b, t_len, chunk, chained)
    return y, k, v, conv_new, h_new, ffn_new


def _last_window(kv, b, t_len):
    return kv.reshape(b, t_len, D_KV)[:, t_len - WINDOW:].reshape(b, WINDOW, N_KV_HEADS, HEAD_DIM)


def _front_pad(state):
    return jnp.pad(state, ((0, 0), (SUBLANES - state.shape[1], 0), (0, 0)))


def kernel(x_prompt, x_sample, state_lru_conv, state_lru_h, cache_k, cache_v, state_ffn_conv, g_mix, w_in, b_gate, w_lru_conv, b_lru_conv, w_lru_a, b_lru_a, w_lru_i, b_lru_i, lru_lambda, g_q, g_k, attn_sink, w_br_lru, w_br_attn, w_out, g_ffn, w_up, w_ffn_conv, b_ffn_conv, w_down):
    params = dict(g_mix=g_mix, w_in=w_in, b_gate=b_gate, w_lru_conv=w_lru_conv, b_lru_conv=b_lru_conv,
                  w_lru_a=w_lru_a, b_lru_a=b_lru_a, w_lru_i=w_lru_i, b_lru_i=b_lru_i, lru_lambda=lru_lambda,
                  g_q=g_q, g_k=g_k, attn_sink=attn_sink, w_br_lru=w_br_lru, w_br_attn=w_br_attn, w_out=w_out,
                  g_ffn=g_ffn, w_up=w_up, w_ffn_conv=w_ffn_conv, b_ffn_conv=b_ffn_conv, w_down=w_down)
    depth = w_in.shape[0]
    bp, sp, _ = x_prompt.shape
    bs, ss, _ = x_sample.shape
    p_chunk, s_chunk = 64, ss
    xp = x_prompt.reshape(bp * sp, D_MODEL)
    xs = x_sample.reshape(bs * ss, D_MODEL)
    zeros_p = (jnp.zeros((bp, SUBLANES, D_LRU), F32), jnp.zeros((bp, 1, D_LRU), F32),
               jnp.zeros((bp, SUBLANES, D_FF), F32))
    outs = {n: [] for n in ("p_lc", "p_lh", "p_k", "p_v", "p_fc", "s_lc", "s_lh", "s_k", "s_v", "s_fc")}
    for l in range(depth):
        lw = _prep_layer(l, params, {p_chunk, s_chunk})
        xp, k, v, lc, lh, fc = _layer(xp, lw, *zeros_p, None, None, bp, sp, p_chunk, True)
        outs["p_lc"].append(lc)
        outs["p_lh"].append(lh.reshape(bp, D_LRU))
        outs["p_k"].append(_last_window(k, bp, sp))
        outs["p_v"].append(_last_window(v, bp, sp))
        outs["p_fc"].append(fc)
        xs, k, v, lc, lh, fc = _layer(
            xs, lw, _front_pad(state_lru_conv[l]), state_lru_h[l], _front_pad(state_ffn_conv[l]),
            cache_k[l].reshape(bs * WINDOW, D_KV), cache_v[l].reshape(bs * WINDOW, D_KV),
            bs, ss, s_chunk, False)
        outs["s_lc"].append(lc.transpose(1, 0, 2))
        outs["s_lh"].append(lh.reshape(bs, D_LRU))
        outs["s_k"].append(k.reshape(bs, ss, N_KV_HEADS, HEAD_DIM))
        outs["s_v"].append(v.reshape(bs, ss, N_KV_HEADS, HEAD_DIM))
        outs["s_fc"].append(fc)
    st = {n: jnp.stack(v) for n, v in outs.items()}
    return (xp.reshape(bp, sp, D_MODEL), xs.reshape(bs, ss, D_MODEL),
            st["p_lc"], st["p_lh"], st["p_k"], st["p_v"], st["p_fc"],
            st["s_lc"], st["s_lh"], st["s_k"], st["s_v"], st["s_fc"])
```

```python
import functools

import jax
import jax.numpy as jnp
from jax import lax
from jax.experimental import pallas as pl
from jax.experimental.pallas import tpu as pltpu

D_MODEL = 1024
N_HEADS = 16
N_KV_HEADS = 4
HEAD_DIM = 64
GROUP = N_HEADS // N_KV_HEADS
D_KV = N_KV_HEADS * HEAD_DIM
WINDOW = 128
D_LRU = D_MODEL
N_LRU_BLOCKS = 16
LRU_BLOCK = D_LRU // N_LRU_BLOCKS
LRU_C = 8.0
LRU_CONV_W = 4
D_FF = 3 * D_MODEL
FFN_CONV_W = 3
EPS = 1e-6
NEG = -1e30

MXU_TILE = 256
SUBLANES = 8
LANES = 128
SEG = 32
SEG_PITCH = SEG + SUBLANES
TINY = 1e-30
LOG2_E = 1.4426950408889634
N_GATE_GROUPS = D_LRU // MXU_TILE

_U0, _Y0, _Q0, _G0, _K0, _V0, _D_IN = 0, 1024, 2048, 3072, 5120, 5376, 5632

VMEM_LIMIT = 56 * 1024 * 1024

BF16 = jnp.bfloat16
F32 = jnp.float32


def _dot(a, b):
    return jnp.dot(a, b, preferred_element_type=F32)


def _rms_scale(x):
    return lax.rsqrt(jnp.mean(x * x, axis=-1, keepdims=True) + EPS)


def _const_spec(shape):
    nd = len(shape)
    return pl.BlockSpec(shape, lambda *_: (0,) * nd, pipeline_mode=pl.Buffered(1))


def _head_norm(t, gain, gn):
    ms = _dot((t * t).astype(BF16), gn)
    return t * lax.rsqrt(ms + EPS) * gain


def _lru_body(u, gy, cs_ref, h0_ref, wc_ref, bc_ref, wai_ref, ba_ref, bi_ref, lam_ref, wbr_ref,
              us, hsl, carry_u, carry_h, *, nseg, chained):
    nlb = D_LRU // LANES
    taps = LRU_CONV_W - 1

    for lb in range(nlb):
        cols = slice(lb * LANES, (lb + 1) * LANES)
        for r in range(nseg):
            base = r * SEG_PITCH
            rows = slice(r * SEG, (r + 1) * SEG)
            if not chained:
                prev = cs_ref[r, :, cols]
            elif r == 0:
                prev = carry_u[:, cols]
            else:
                prev = u[r * SEG - SUBLANES:r * SEG, cols]
            us[lb, base:base + SUBLANES, :] = prev
            us[lb, base + SUBLANES:base + SEG_PITCH, :] = u[rows, cols]

    def seg_rows(i):
        return pl.ds(SUBLANES + i, nseg, stride=SEG_PITCH)

    xc_lb, u_tail = [], []
    for lb in range(nlb):
        cols = slice(lb * LANES, (lb + 1) * LANES)
        w = [wc_ref[k, 0:nseg, cols] for k in range(LRU_CONV_W)]
        bias = bc_ref[0:nseg, cols]
        ut = [us[lb, seg_rows(i - taps), :] for i in range(SEG + taps)]
        steps = []
        for j in range(SEG):
            x = bias
            for k in range(LRU_CONV_W):
                x = x + w[k] * ut[j + k]
            steps.append(x)
        xc_lb.append(jnp.concatenate(steps, axis=0))
        u_tail.append(ut[SEG:])
    xc = jnp.concatenate(xc_lb, axis=1)

    lam = lam_ref[...]
    softplus_neg_lam = jnp.maximum(-lam, 0.0) + jnp.log1p(jnp.exp(-jnp.abs(lam)))
    log2_a_per_r = (-LRU_C * LOG2_E) * softplus_neg_lam
    xcb = xc.astype(BF16)
    a_lb, b_lb = [], []
    for j in range(N_GATE_GROUPS):
        cols = slice(j * MXU_TILE, (j + 1) * MXU_TILE)
        ri = _dot(xcb[:, cols], wai_ref[j])
        r = jax.nn.sigmoid(ri[:, :MXU_TILE] + ba_ref[:, cols])
        i = jax.nn.sigmoid(ri[:, MXU_TILE:] + bi_ref[:, cols])
        a = jnp.exp2(r * log2_a_per_r[:, cols])
        one_minus_a2 = 1.0 - a * a
        mult = one_minus_a2 * lax.rsqrt(jnp.maximum(one_minus_a2, TINY))
        b = mult * i * xc[:, cols]
        for k in range(MXU_TILE // LANES):
            a_lb.append(a[:, k * LANES:(k + 1) * LANES])
            b_lb.append(b[:, k * LANES:(k + 1) * LANES])

    h_last, c_next = [], []
    for lb in range(nlb):
        cols = slice(lb * LANES, (lb + 1) * LANES)
        step = lambda v, j: v[j * nseg:(j + 1) * nseg, :]
        if chained:
            hs, ap = jnp.zeros((nseg, LANES), F32), jnp.ones((nseg, LANES), F32)
            hs_l, ap_l = [], []
            for j in range(SEG):
                a = step(a_lb[lb], j)
                hs = a * hs + step(b_lb[lb], j)
                ap = a * ap
                hs_l.append(hs)
                ap_l.append(ap)
            c = carry_h[:, cols]
            cin = []
            for r in range(nseg):
                cin.append(c)
                c = ap[r:r + 1, :] * c + hs[r:r + 1, :]
            c_next.append(c)
            cin = jnp.concatenate(cin, axis=0)
            h_l = [hs_l[j] + ap_l[j] * cin for j in range(SEG)]
        else:
            h, h_l = h0_ref[:, cols], []
            for j in range(SEG):
                h = step(a_lb[lb], j) * h + step(b_lb[lb], j)
                h_l.append(h)
            h_last.append(h)
        for j in range(SEG):
            hsl[lb, seg_rows(j), :] = h_l[j]

    h_nat = jnp.concatenate(
        [jnp.concatenate([hsl[lb, r * SEG_PITCH + SUBLANES:(r + 1) * SEG_PITCH, :] for r in range(nseg)], axis=0)
         for lb in range(nlb)], axis=1)
    br_nat = _dot((h_nat * gy).astype(BF16), wbr_ref[...])

    if chained:
        return br_nat, None, jnp.concatenate(c_next, axis=1)
    conv_tail = [jnp.concatenate([u_tail[lb][k] for lb in range(nlb)], axis=1) for k in range(taps)]
    return br_nat, conv_tail, jnp.concatenate(h_last, axis=1)


def _proj_lru_kernel(x_ref, gmix_ref, w_ref, bg_ref, gq_ref, gk_ref, gn_ref,
                     cs_ref, h0_ref, wc_ref, bc_ref, wai_ref, ba_ref, bi_ref, lam_ref, wbr_ref,
                     z_ref, q_ref, g1_ref, k_ref, v_ref, cso_ref, ho_ref,
                     us, hsl, carry_u, carry_h, *, nseg, chained):
    t = pl.program_id(1)
    tm = nseg * SEG
    taps = LRU_CONV_W - 1
    if chained:
        @pl.when(t == 0)
        def _():
            carry_u[...] = cs_ref[...]
            carry_h[...] = h0_ref[...]

    x = x_ref[...]
    xn = (x * _rms_scale(x) * gmix_ref[...]).astype(BF16)

    def proj(lo, hi):
        return _dot(xn, w_ref[:, lo:hi])

    u = proj(_U0, _Y0)
    gy = jax.nn.gelu(proj(_Y0, _Q0))
    br, conv_tail, h_end = _lru_body(
        u, gy, cs_ref, h0_ref, wc_ref, bc_ref, wai_ref, ba_ref, bi_ref, lam_ref, wbr_ref,
        us, hsl, carry_u, carry_h, nseg=nseg, chained=chained)
    gates = jax.nn.sigmoid(proj(_G0, _K0) + bg_ref[...])
    z_ref[...] = gates[:, :D_MODEL] * br
    g1_ref[...] = gates[:, D_MODEL:].astype(BF16)

    q = proj(_Q0, _G0)
    kv = proj(_K0, _D_IN)
    scale = HEAD_DIM ** -0.5
    for j in range(D_MODEL // MXU_TILE):
        cols = slice(j * MXU_TILE, (j + 1) * MXU_TILE)
        q_ref[:, cols] = (_head_norm(q[:, cols], gq_ref[...], gn_ref[...]) * scale).astype(BF16)
    k_ref[...] = _head_norm(kv[:, :D_KV], gk_ref[...], gn_ref[...])
    v_ref[...] = kv[:, D_KV:]

    if chained:
        carry_u[...] = u[tm - SUBLANES:tm, :]
        carry_h[...] = h_end

        @pl.when(t == pl.num_programs(1) - 1)
        def _():
            cso_ref[...] = u[tm - taps:tm, :]
            ho_ref[...] = h_end
    else:
        for k in range(taps):
            cso_ref[k] = conv_tail[k]
        ho_ref[...] = h_end


def _proj_lru(x, conv_state, h_state, lw, b, t_len, chained):
    taps = LRU_CONV_W - 1
    if chained:
        nseg = SUBLANES
        tm = nseg * SEG
        nt = t_len // tm
        grid = (b, nt)
        row = lambda w: pl.BlockSpec((tm, w), lambda bi, ti: (bi * nt + ti, 0))
        per_b = lambda r: pl.BlockSpec((None, r, D_LRU), lambda bi, ti: (bi, 0, 0))
        state_specs = [per_b(SUBLANES), per_b(1)]
        out_state_specs = [per_b(taps), per_b(1)]
        out_state_shapes = [jax.ShapeDtypeStruct((b, taps, D_LRU), F32), jax.ShapeDtypeStruct((b, 1, D_LRU), F32)]
    else:
        assert t_len == SEG
        nseg = SUBLANES if b % SUBLANES == 0 else b
        tm = nseg * SEG
        grid = (b // nseg, 1)
        row = lambda w: pl.BlockSpec((tm, w), lambda bi, ti: (bi, 0))
        state_specs = [pl.BlockSpec((nseg, SUBLANES, D_LRU), lambda bi, ti: (bi, 0, 0)),
                       pl.BlockSpec((nseg, D_LRU), lambda bi, ti: (bi, 0))]
        out_state_specs = [pl.BlockSpec((taps, nseg, D_LRU), lambda bi, ti: (0, bi, 0)),
                           pl.BlockSpec((nseg, D_LRU), lambda bi, ti: (bi, 0))]
        out_state_shapes = [jax.ShapeDtypeStruct((taps, b, D_LRU), F32), jax.ShapeDtypeStruct((b, D_LRU), F32)]
    m = b * t_len
    vec = _const_spec((1, D_LRU))
    slab = pltpu.VMEM((D_LRU // LANES, nseg * SEG_PITCH, LANES), F32)
    kern = functools.partial(_proj_lru_kernel, nseg=nseg, chained=chained)
    return pl.pallas_call(
        kern,
        grid=grid,
        in_specs=[row(D_MODEL), _const_spec((1, D_MODEL)), _const_spec((D_MODEL, _D_IN)),
                  _const_spec((1, 2 * D_MODEL)), _const_spec((1, MXU_TILE)), _const_spec((1, MXU_TILE)),
                  _const_spec((MXU_TILE, MXU_TILE))] + state_specs
                 + [_const_spec((LRU_CONV_W, SUBLANES, D_LRU)), _const_spec((SUBLANES, D_LRU)),
                    _const_spec((N_GATE_GROUPS, MXU_TILE, 2 * MXU_TILE)), vec, vec, vec,
                    _const_spec((D_LRU, D_MODEL))],
        out_specs=[row(D_MODEL), row(D_MODEL), row(D_MODEL), row(D_KV), row(D_KV)] + out_state_specs,
        out_shape=[jax.ShapeDtypeStruct((m, D_MODEL), F32), jax.ShapeDtypeStruct((m, D_MODEL), BF16),
                   jax.ShapeDtypeStruct((m, D_MODEL), BF16), jax.ShapeDtypeStruct((m, D_KV), F32),
                   jax.ShapeDtypeStruct((m, D_KV), F32)] + out_state_shapes,
        scratch_shapes=[slab, slab, pltpu.VMEM((SUBLANES, D_LRU), F32), pltpu.VMEM((1, D_LRU), F32)],
        compiler_params=pltpu.CompilerParams(vmem_limit_bytes=VMEM_LIMIT),
        name="proj_lru",
    )(x, lw["g_mix"], lw["w_in"], lw["b_gate"], lw["g_q"], lw["g_k"], lw["gn"], conv_state, h_state,
      lw["w_lconv"], lw["b_lconv"], lw["w_ai"], lw["b_a"], lw["b_i"], lw["lam"], lw["w_br_lru"])


def _attn_ffn_kernel(q_ref, kh_ref, vh_ref, kt_ref, vt_ref, fill_ref, z_ref, g1_ref, x_ref,
                     wba_ref, wout_ref, gffn_ref, fs_ref, wup_ref, wfc_ref, bfc_ref, wdn_ref,
                     y_ref, fso_ref, kbuf, vbuf, attn_s, gbuf, *, tq, chunk, chained):
    ti = pl.program_id(1)
    nunits = tq // chunk
    win = WINDOW + chunk
    pad = MXU_TILE - win
    col_head = lax.broadcasted_iota(jnp.int32, (1, D_KV), 1) // HEAD_DIM
    for src_h, src_t, buf in ((kh_ref, kt_ref, kbuf), (vh_ref, vt_ref, vbuf)):
        hist, rows = src_h[...].astype(BF16), src_t[...].astype(BF16)
        for kh in range(N_KV_HEADS):
            m = (col_head == kh).astype(BF16)
            if chained:
                buf[kh, 0:WINDOW, :] = hist * m
                buf[kh, WINDOW:, :] = rows * m
            else:
                for s in range(nunits):
                    buf[kh, s * win:s * win + WINDOW, :] = hist[s * WINDOW:(s + 1) * WINDOW, :] * m
                    buf[kh, s * win + WINDOW:(s + 1) * win, :] = rows[s * chunk:(s + 1) * chunk, :] * m

    zero_keys = jnp.zeros((pad, D_KV), BF16)
    key_pos = lax.broadcasted_iota(jnp.int32, (1, MXU_TILE), 1)
    for c in range(nunits):
        r0 = c * chunk
        k0 = r0 if chained else c * win
        qst = jnp.concatenate(
            [q_ref[r0:r0 + chunk, g * D_KV:(g + 1) * D_KV] for g in range(GROUP)], axis=0)

        def keys(buf):
            return jnp.concatenate(
                [blk for kh in range(N_KV_HEADS) for blk in (zero_keys, buf[kh, k0:k0 + win, :])], axis=0)

        s_all = lax.dot_general(qst, keys(kbuf), (((1,), (1,)), ((), ())), preferred_element_type=F32)
        valid = key_pos >= pad
        if chained and r0 < WINDOW:
            valid = jnp.logical_and(valid, jnp.logical_or(key_pos >= pad + WINDOW - r0, ti > 0))
        probs = []
        for kh in range(N_KV_HEADS):
            s = jnp.where(valid, s_all[:, kh * MXU_TILE:(kh + 1) * MXU_TILE], fill_ref[kh])
            e = jnp.exp(s - jnp.max(s, axis=-1, keepdims=True))
            probs.append((e * (1.0 / jnp.sum(e, axis=-1, keepdims=True))).astype(BF16))
        o = _dot(jnp.concatenate(probs, axis=1), keys(vbuf))
        for g in range(GROUP):
            attn_s[r0:r0 + chunk, g * D_KV:(g + 1) * D_KV] = o[g * chunk:(g + 1) * chunk, :].astype(BF16)

    br = _dot(attn_s[...], wba_ref[...])
    mix = (z_ref[...] + g1_ref[...].astype(F32) * br).astype(BF16)
    h = x_ref[...] + _dot(mix, wout_ref[...])
    hn = (h * _rms_scale(h) * gffn_ref[...]).astype(BF16)

    taps = FFN_CONV_W - 1
    nstreams = 1 if chained else nunits
    seg = tq // nstreams
    pitch = seg + SUBLANES
    if chained:
        @pl.when(ti == 0)
        def _():
            gbuf[0:SUBLANES, :] = fs_ref[...]
    gate = _dot(hn, wup_ref[:, :D_FF])
    up = _dot(hn, wup_ref[:, D_FF:])
    parts = []
    for r in range(nstreams):
        base = r * pitch
        if not chained:
            gbuf[base:base + SUBLANES, :] = fs_ref[r]
        gbuf[base + SUBLANES:base + pitch, :] = gate[r * seg:(r + 1) * seg, :]
        gc = bfc_ref[...]
        for j in range(FFN_CONV_W):
            lo = base + SUBLANES - taps + j
            gc = gc + wfc_ref[j:j + 1, :] * gbuf[lo:lo + seg, :]
        parts.append(gc)
    gc = parts[0] if nstreams == 1 else jnp.concatenate(parts, axis=0)
    act = (jax.nn.gelu(gc) * up).astype(BF16)
    y_ref[...] = h + _dot(act, wdn_ref[...])

    if chained:
        gbuf[0:SUBLANES, :] = gbuf[tq:tq + SUBLANES, :]

        @pl.when(ti == pl.num_programs(1) - 1)
        def _():
            fso_ref[...] = gbuf[SUBLANES - taps:SUBLANES, :]
    else:
        for r in range(nstreams):
            fso_ref[r] = gbuf[(r + 1) * pitch - taps:(r + 1) * pitch, :]


def _attn_ffn(q, k, v, k_hist, v_hist, z, g1, x, ffn_state, lw, b, t_len, chunk, chained):
    taps = FFN_CONV_W - 1
    if chained:
        tq, ns = _pick(t_len, 256), 1
        nt = t_len // tq
        grid = (b, nt)
        row = lambda w: pl.BlockSpec((tq, w), lambda bi, ti: (bi * nt + ti, 0))
        per_b, per_t = t_len // WINDOW, tq // WINDOW
        hist = pl.BlockSpec((WINDOW, D_KV), lambda bi, ti: (jnp.maximum(bi * per_b + ti * per_t - 1, 0), 0))
        kv_rows = WINDOW + tq
        state = lambda r: pl.BlockSpec((None, r, D_FF), lambda bi, ti: (bi, 0, 0))
    else:
        assert t_len == chunk
        ns = SUBLANES if b % SUBLANES == 0 else b
        tq = ns * chunk
        grid = (b // ns, 1)
        row = lambda w: pl.BlockSpec((tq, w), lambda bi, ti: (bi, 0))
        hist = pl.BlockSpec((ns * WINDOW, D_KV), lambda bi, ti: (bi, 0))
        kv_rows = ns * (WINDOW + chunk)
        state = lambda r: pl.BlockSpec((ns, r, D_FF), lambda bi, ti: (bi, 0, 0))
    kern = functools.partial(_attn_ffn_kernel, tq=tq, chunk=chunk, chained=chained)
    return pl.pallas_call(
        kern,
        grid=grid,
        in_specs=[row(D_MODEL), hist, hist, row(D_KV), row(D_KV),
                  _const_spec((N_KV_HEADS, GROUP * chunk, MXU_TILE)),
                  row(D_MODEL), row(D_MODEL), row(D_MODEL),
                  _const_spec((D_MODEL, D_MODEL)), _const_spec((D_MODEL, D_MODEL)), _const_spec((1, D_MODEL)),
                  state(SUBLANES), _const_spec((D_MODEL, 2 * D_FF)),
                  _const_spec((FFN_CONV_W, D_FF)), _const_spec((1, D_FF)), _const_spec((D_FF, D_MODEL))],
        out_specs=[row(D_MODEL), state(taps)],
        out_shape=[jax.ShapeDtypeStruct((b * t_len, D_MODEL), F32),
                   jax.ShapeDtypeStruct((b, taps, D_FF), F32)],
        scratch_shapes=[pltpu.VMEM((N_KV_HEADS, kv_rows, D_KV), BF16)] * 2
                       + [pltpu.VMEM((tq, D_MODEL), BF16),
                          pltpu.VMEM((ns * (tq // ns + SUBLANES), D_FF), F32)],
        compiler_params=pltpu.CompilerParams(vmem_limit_bytes=VMEM_LIMIT),
        name="attn_ffn",
    )(q, k_hist, v_hist, k, v, lw["sink_fill"][chunk], z, g1, x, lw["w_br_attn"], lw["w_out"], lw["g_ffn"],
      ffn_state, lw["w_up"], lw["w_fconv"], lw["b_fconv"], lw["w_down"])


def _block_diag_groups(w):
    per = MXU_TILE // LRU_BLOCK
    w4 = w.reshape(N_GATE_GROUPS, per, LRU_BLOCK, LRU_BLOCK)
    return jnp.einsum("jncd,nm->jncmd", w4, jnp.eye(per, dtype=w.dtype)).reshape(
        N_GATE_GROUPS, MXU_TILE, MXU_TILE)


def _prep_layer(l, p, chunks):
    w_in = p["w_in"][l]
    wq = w_in[:, 2048:3072].reshape(D_MODEL, N_KV_HEADS, GROUP, HEAD_DIM).transpose(0, 2, 1, 3)
    w_in_p = jnp.concatenate(
        [w_in[:, :2048], wq.reshape(D_MODEL, D_MODEL), w_in[:, 3584:], w_in[:, 3072:3584]], axis=1)
    w_ba = p["w_br_attn"][l].reshape(N_KV_HEADS, GROUP, HEAD_DIM, D_MODEL).transpose(1, 0, 2, 3)
    per = MXU_TILE // HEAD_DIM
    sink = p["attn_sink"][l].reshape(N_KV_HEADS, GROUP)
    row = lambda v: v.reshape(1, -1)
    return {
        "g_mix": row(p["g_mix"][l]),
        "w_in": w_in_p.astype(BF16),
        "b_gate": row(p["b_gate"][l]),
        "g_q": row(jnp.tile(p["g_q"][l], per)),
        "g_k": row(jnp.tile(p["g_k"][l], per)),
        "gn": (jnp.kron(jnp.eye(per, dtype=F32), jnp.ones((HEAD_DIM, HEAD_DIM), F32)) / HEAD_DIM).astype(BF16),
        "w_lconv": jnp.broadcast_to(p["w_lru_conv"][l][:, None, :], (LRU_CONV_W, SUBLANES, D_LRU)),
        "b_lconv": jnp.broadcast_to(p["b_lru_conv"][l][None, :], (SUBLANES, D_LRU)),
        "w_ai": jnp.concatenate([_block_diag_groups(p["w_lru_a"][l]), _block_diag_groups(p["w_lru_i"][l])],
                                axis=2).astype(BF16),
        "b_a": row(p["b_lru_a"][l]),
        "b_i": row(p["b_lru_i"][l]),
        "lam": row(p["lru_lambda"][l]),
        "w_br_lru": p["w_br_lru"][l].astype(BF16),
        "sink_fill": {c: jnp.full((N_KV_HEADS, GROUP * c, MXU_TILE), NEG, F32).at[:, :, 0].set(
            jnp.repeat(sink, c, axis=1)) for c in chunks},
        "w_br_attn": w_ba.reshape(D_MODEL, D_MODEL).astype(BF16),
        "w_out": p["w_out"][l].astype(BF16),
        "g_ffn": row(p["g_ffn"][l]),
        "w_up": p["w_up"][l].astype(BF16),
        "w_fconv": p["w_ffn_conv"][l],
        "b_fconv": row(p["b_ffn_conv"][l]),
        "w_down": p["w_down"][l].astype(BF16),
    }


def _pick(n, pref):
    return pref if n % pref == 0 else n


def _layer(x, lw, conv_state, h_state, ffn_state, k_hist, v_hist, b, t_len, chunk, chained):
    z, q, g1, k, v, conv_new, h_new = _proj_lru(x, conv_state, h_state, lw, b, t_len, chained)
    if chained:
        k_hist, v_hist = k, v
    y, ffn_new = _attn_ffn(q, k, v, k_hist, v_hist, z, g1, x, ffn_state, lw, b, t_len, chunk, chained)
    return y, k, v, conv_new, h_new, ffn_new


def _last_window(kv, b, t_len):
    return kv.reshape(b, t_len, D_KV)[:, t_len - WINDOW:].reshape(b, WINDOW, N_KV_HEADS, HEAD_DIM)


def _front_pad(state):
    return jnp.pad(state, ((0, 0), (SUBLANES - state.shape[1], 0), (0, 0)))


def kernel(x_prompt, x_sample, state_lru_conv, state_lru_h, cache_k, cache_v, state_ffn_conv, g_mix, w_in, b_gate, w_lru_conv, b_lru_conv, w_lru_a, b_lru_a, w_lru_i, b_lru_i, lru_lambda, g_q, g_k, attn_sink, w_br_lru, w_br_attn, w_out, g_ffn, w_up, w_ffn_conv, b_ffn_conv, w_down):
    params = dict(g_mix=g_mix, w_in=w_in, b_gate=b_gate, w_lru_conv=w_lru_conv, b_lru_conv=b_lru_conv,
                  w_lru_a=w_lru_a, b_lru_a=b_lru_a, w_lru_i=w_lru_i, b_lru_i=b_lru_i, lru_lambda=lru_lambda,
                  g_q=g_q, g_k=g_k, attn_sink=attn_sink, w_br_lru=w_br_lru, w_br_attn=w_br_attn, w_out=w_out,
                  g_ffn=g_ffn, w_up=w_up, w_ffn_conv=w_ffn_conv, b_ffn_conv=b_ffn_conv, w_down=w_down)
    depth = w_in.shape[0]
    bp, sp, _ = x_prompt.shape
    bs, ss, _ = x_sample.shape
    p_chunk, s_chunk = 64, ss
    xp = x_prompt.reshape(bp * sp, D_MODEL)
    xs = x_sample.reshape(bs * ss, D_MODEL)
    zeros_p = (jnp.zeros((bp, SUBLANES, D_LRU), F32), jnp.zeros((bp, 1, D_LRU), F32),
               jnp.zeros((bp, SUBLANES, D_FF), F32))
    outs = {n: [] for n in ("p_lc", "p_lh", "p_k", "p_v", "p_fc", "s_lc", "s_lh", "s_k", "s_v", "s_fc")}
    for l in range(depth):
        lw = _prep_layer(l, params, {p_chunk, s_chunk})
        xp, k, v, lc, lh, fc = _layer(xp, lw, *zeros_p, None, None, bp, sp, p_chunk, True)
        outs["p_lc"].append(lc)
        outs["p_lh"].append(lh.reshape(bp, D_LRU))
        outs["p_k"].append(_last_window(k, bp, sp))
        outs["p_v"].append(_last_window(v, bp, sp))
        outs["p_fc"].append(fc)
        xs, k, v, lc, lh, fc = _layer(
            xs, lw, _front_pad(state_lru_conv[l]), state_lru_h[l], _front_pad(state_ffn_conv[l]),
            cache_k[l].reshape(bs * WINDOW, D_KV), cache_v[l].reshape(bs * WINDOW, D_KV),
            bs, ss, s_chunk, False)
        outs["s_lc"].append(lc.transpose(1, 0, 2))
        outs["s_lh"].append(lh.reshape(bs, D_LRU))
        outs["s_k"].append(k.reshape(bs, ss, N_KV_HEADS, HEAD_DIM))
        outs["s_v"].append(v.reshape(bs, ss, N_KV_HEADS, HEAD_DIM))
        outs["s_fc"].append(fc)
    st = {n: jnp.stack(v) for n, v in outs.items()}
    return (xp.reshape(bp, sp, D_MODEL), xs.reshape(bs, ss, D_MODEL),
            st["p_lc"], st["p_lh"], st["p_k"], st["p_v"], st["p_fc"],
            st["s_lc"], st["s_lh"], st["s_k"], st["s_v"], st["s_fc"])
```

```python
import functools

import jax
import jax.numpy as jnp
from jax import lax
from jax.experimental import pallas as pl
from jax.experimental.pallas import tpu as pltpu

D_MODEL = 1024
N_HEADS = 16
N_KV_HEADS = 4
HEAD_DIM = 64
GROUP = N_HEADS // N_KV_HEADS
D_KV = N_KV_HEADS * HEAD_DIM
WINDOW = 128
D_LRU = D_MODEL
N_LRU_BLOCKS = 16
LRU_BLOCK = D_LRU // N_LRU_BLOCKS
LRU_C = 8.0
LRU_CONV_W = 4
D_FF = 3 * D_MODEL
FFN_CONV_W = 3
EPS = 1e-6
NEG = -1e30

MXU_TILE = 256
SUBLANES = 8
LANES = 128
SEG = 32
ROW_TILE = 256
PROMPT_CHUNK = 64
SEG_PITCH = SEG + SUBLANES
TINY = 1e-30
N_GATE_GROUPS = D_LRU // MXU_TILE

_U0, _Y0, _Q0, _G0, _K0, _D_IN = 0, 1024, 2048, 3072, 5120, 5632

VMEM_LIMIT = 56 * 1024 * 1024

BF16 = jnp.bfloat16
F32 = jnp.float32


def _dot(a, b):
    return jnp.dot(a, b, preferred_element_type=F32)


def _rms_scale(x):
    return lax.rsqrt(jnp.mean(x * x, axis=-1, keepdims=True) + EPS)


def _const_spec(shape):
    nd = len(shape)
    return pl.BlockSpec(shape, lambda *_: (0,) * nd, pipeline_mode=pl.Buffered(1))


def _head_norm(t, gain, gn):
    ms = _dot((t * t).astype(BF16), gn)
    return t * lax.rsqrt(ms + EPS) * gain


def _lru_body(u, gy_fn, cs_ref, h0_ref, wc_ref, bc_ref, wai_ref, ba_ref, bi_ref, lam_ref, wbr_ref,
              us, hsl, carry_u, carry_h, *, nseg, chained):
    nlb = D_LRU // LANES
    taps = LRU_CONV_W - 1

    for lb in range(nlb):
        cols = slice(lb * LANES, (lb + 1) * LANES)
        for r in range(nseg):
            base = r * SEG_PITCH
            rows = slice(r * SEG, (r + 1) * SEG)
            if not chained:
                prev = cs_ref[r, :, cols]
            elif r == 0:
                prev = carry_u[:, cols]
            else:
                prev = u[r * SEG - SUBLANES:r * SEG, cols]
            us[lb, base:base + SUBLANES, :] = prev
            us[lb, base + SUBLANES:base + SEG_PITCH, :] = u[rows, cols]

    def seg_rows(i):
        return pl.ds(SUBLANES + i, nseg, stride=SEG_PITCH)

    xc_lb, u_tail = [], []
    for lb in range(nlb):
        cols = slice(lb * LANES, (lb + 1) * LANES)
        w = [wc_ref[k, 0:nseg, cols] for k in range(LRU_CONV_W)]
        bias = bc_ref[0:nseg, cols]
        ut = [us[lb, seg_rows(i - taps), :] for i in range(SEG + taps)]
        steps = []
        for j in range(SEG):
            x = bias
            for k in range(LRU_CONV_W):
                x = x + w[k] * ut[j + k]
            steps.append(x)
        xc_lb.append(jnp.concatenate(steps, axis=0))
        u_tail.append(ut[SEG:])
    xc = jnp.concatenate(xc_lb, axis=1)

    lam = lam_ref[...]
    softplus_neg_lam = jnp.maximum(-lam, 0.0) + jnp.log1p(jnp.exp(-jnp.abs(lam)))
    log_a_per_r = -LRU_C * softplus_neg_lam
    xcb = xc.astype(BF16)
    a_lb, b_lb = [], []
    for j in range(N_GATE_GROUPS):
        cols = slice(j * MXU_TILE, (j + 1) * MXU_TILE)
        ri = _dot(xcb[:, cols], wai_ref[j])
        r = jax.nn.sigmoid(ri[:, :MXU_TILE] + ba_ref[:, cols])
        i = jax.nn.sigmoid(ri[:, MXU_TILE:] + bi_ref[:, cols])
        log_a = r * log_a_per_r[:, cols]
        a = jnp.exp(log_a)
        one_minus_a2 = -jnp.tanh(log_a) * (a * a + 1.0)
        mult = one_minus_a2 * lax.rsqrt(jnp.maximum(one_minus_a2, TINY))
        b = mult * i * xc[:, cols]
        for k in range(MXU_TILE // LANES):
            a_lb.append(a[:, k * LANES:(k + 1) * LANES])
            b_lb.append(b[:, k * LANES:(k + 1) * LANES])

    h_last, c_next = [], []
    for lb in range(nlb):
        cols = slice(lb * LANES, (lb + 1) * LANES)
        step = lambda v, j: v[j * nseg:(j + 1) * nseg, :]
        if chained:
            hs, ap = jnp.zeros((nseg, LANES), F32), jnp.ones((nseg, LANES), F32)
            hs_l, ap_l = [], []
            for j in range(SEG):
                a = step(a_lb[lb], j)
                hs = a * hs + step(b_lb[lb], j)
                ap = a * ap
                hs_l.append(hs)
                ap_l.append(ap)
            c = carry_h[:, cols]
            cin = []
            for r in range(nseg):
                cin.append(c)
                c = ap[r:r + 1, :] * c + hs[r:r + 1, :]
            c_next.append(c)
            cin = jnp.concatenate(cin, axis=0)
            h_l = [hs_l[j] + ap_l[j] * cin for j in range(SEG)]
        else:
            h, h_l = h0_ref[:, cols], []
            for j in range(SEG):
                h = step(a_lb[lb], j) * h + step(b_lb[lb], j)
                h_l.append(h)
            h_last.append(h)
        for j in range(SEG):
            hsl[lb, seg_rows(j), :] = h_l[j]

    h_nat = jnp.concatenate(
        [jnp.concatenate([hsl[lb, r * SEG_PITCH + SUBLANES:(r + 1) * SEG_PITCH, :] for r in range(nseg)], axis=0)
         for lb in range(nlb)], axis=1)
    br_nat = _dot((h_nat * gy_fn()).astype(BF16), wbr_ref[...])

    if chained:
        return br_nat, None, jnp.concatenate(c_next, axis=1)
    conv_tail = [jnp.concatenate([u_tail[lb][k] for lb in range(nlb)], axis=1) for k in range(taps)]
    return br_nat, conv_tail, jnp.concatenate(h_last, axis=1)


def _proj_lru_kernel(x_ref, gmix_ref, w_ref, bg_ref, gq_ref, gk_ref, gn_ref,
                     cs_ref, h0_ref, wc_ref, bc_ref, wai_ref, ba_ref, bi_ref, lam_ref, wbr_ref,
                     z_ref, q_ref, g1_ref, k_ref, v_ref, cso_ref, ho_ref,
                     us, hsl, carry_u, carry_h, *, nseg, chained):
    t = pl.program_id(1)
    tm = nseg * SEG
    taps = LRU_CONV_W - 1
    if chained:
        @pl.when(t == 0)
        def _():
            carry_u[...] = cs_ref[...]
            carry_h[...] = h0_ref[...]

    x = x_ref[...]
    xn = (x * _rms_scale(x) * gmix_ref[...]).astype(BF16)

    def proj(lo, hi):
        return _dot(xn, w_ref[:, lo:hi])

    u = proj(_U0, _Y0)
    br, conv_tail, h_end = _lru_body(
        u, lambda: jax.nn.gelu(proj(_Y0, _Q0)), cs_ref, h0_ref, wc_ref, bc_ref, wai_ref, ba_ref, bi_ref, lam_ref, wbr_ref,
        us, hsl, carry_u, carry_h, nseg=nseg, chained=chained)
    gates = jax.nn.sigmoid(proj(_G0, _K0) + bg_ref[...])
    z_ref[...] = gates[:, :D_MODEL] * br
    g1_ref[...] = gates[:, D_MODEL:].astype(BF16)

    q = proj(_Q0, _G0)
    kv = proj(_K0, _D_IN)
    scale = HEAD_DIM ** -0.5
    for j in range(D_MODEL // MXU_TILE):
        cols = slice(j * MXU_TILE, (j + 1) * MXU_TILE)
        q_ref[:, cols] = (_head_norm(q[:, cols], gq_ref[...], gn_ref[...]) * scale).astype(BF16)
    k_ref[...] = _head_norm(kv[:, :D_KV], gk_ref[...], gn_ref[...])
    v_ref[...] = kv[:, D_KV:]

    if chained:
        carry_u[...] = u[tm - SUBLANES:tm, :]
        carry_h[...] = h_end

        @pl.when(t == pl.num_programs(1) - 1)
        def _():
            cso_ref[...] = u[tm - taps:tm, :]
            ho_ref[...] = h_end
    else:
        for k in range(taps):
            cso_ref[k] = conv_tail[k]
        ho_ref[...] = h_end


def _proj_lru(x, conv_state, h_state, lw, b, t_len, chained):
    taps = LRU_CONV_W - 1
    if chained:
        nseg = SUBLANES
        tm = nseg * SEG
        nt = t_len // tm
        grid = (b, nt)
        row = lambda w: pl.BlockSpec((tm, w), lambda bi, ti: (bi * nt + ti, 0))
        per_b = lambda r: pl.BlockSpec((None, r, D_LRU), lambda bi, ti: (bi, 0, 0))
        state_specs = [per_b(SUBLANES), per_b(1)]
        out_state_specs = [per_b(taps), per_b(1)]
        out_state_shapes = [jax.ShapeDtypeStruct((b, taps, D_LRU), F32), jax.ShapeDtypeStruct((b, 1, D_LRU), F32)]
    else:
        assert t_len == SEG
        nseg = SUBLANES if b % SUBLANES == 0 else b
        tm = nseg * SEG
        grid = (b // nseg, 1)
        row = lambda w: pl.BlockSpec((tm, w), lambda bi, ti: (bi, 0))
        state_specs = [pl.BlockSpec((nseg, SUBLANES, D_LRU), lambda bi, ti: (bi, 0, 0)),
                       pl.BlockSpec((nseg, D_LRU), lambda bi, ti: (bi, 0))]
        out_state_specs = [pl.BlockSpec((taps, nseg, D_LRU), lambda bi, ti: (0, bi, 0)),
                           pl.BlockSpec((nseg, D_LRU), lambda bi, ti: (bi, 0))]
        out_state_shapes = [jax.ShapeDtypeStruct((taps, b, D_LRU), F32), jax.ShapeDtypeStruct((b, D_LRU), F32)]
    m = b * t_len
    vec = _const_spec((1, D_LRU))
    slab = pltpu.VMEM((D_LRU // LANES, nseg * SEG_PITCH, LANES), F32)
    kern = functools.partial(_proj_lru_kernel, nseg=nseg, chained=chained)
    return pl.pallas_call(
        kern,
        grid=grid,
        in_specs=[row(D_MODEL), _const_spec((1, D_MODEL)), _const_spec((D_MODEL, _D_IN)),
                  _const_spec((1, 2 * D_MODEL)), _const_spec((1, MXU_TILE)), _const_spec((1, MXU_TILE)),
                  _const_spec((MXU_TILE, MXU_TILE))] + state_specs
                 + [_const_spec((LRU_CONV_W, SUBLANES, D_LRU)), _const_spec((SUBLANES, D_LRU)),
                    _const_spec((N_GATE_GROUPS, MXU_TILE, 2 * MXU_TILE)), vec, vec, vec,
                    _const_spec((D_LRU, D_MODEL))],
        out_specs=[row(D_MODEL), row(D_MODEL), row(D_MODEL), row(D_KV), row(D_KV)] + out_state_specs,
        out_shape=[jax.ShapeDtypeStruct((m, D_MODEL), F32), jax.ShapeDtypeStruct((m, D_MODEL), BF16),
                   jax.ShapeDtypeStruct((m, D_MODEL), BF16), jax.ShapeDtypeStruct((m, D_KV), F32),
                   jax.ShapeDtypeStruct((m, D_KV), F32)] + out_state_shapes,
        scratch_shapes=[slab, slab, pltpu.VMEM((SUBLANES, D_LRU), F32), pltpu.VMEM((1, D_LRU), F32)],
        compiler_params=pltpu.CompilerParams(vmem_limit_bytes=VMEM_LIMIT),
        name="proj_lru",
    )(x, lw["g_mix"], lw["w_in"], lw["b_gate"], lw["g_q"], lw["g_k"], lw["gn"], conv_state, h_state,
      lw["w_lconv"], lw["b_lconv"], lw["w_ai"], lw["b_a"], lw["b_i"], lw["lam"], lw["w_br_lru"])


def _attn_ffn_kernel(q_ref, kh_ref, vh_ref, kt_ref, vt_ref, fill_ref, z_ref, g1_ref, x_ref,
                     wba_ref, wout_ref, gffn_ref, fs_ref, wup_ref, wfc_ref, bfc_ref, wdn_ref,
                     y_ref, fso_ref, kbuf, vbuf, attn_s, gbuf, *, tq, chunk, chained):
    ti = pl.program_id(1)
    nunits = tq // chunk
    win = WINDOW + chunk
    pad = MXU_TILE - win
    col_head = lax.broadcasted_iota(jnp.int32, (1, D_KV), 1) // HEAD_DIM
    for src_h, src_t, buf in ((kh_ref, kt_ref, kbuf), (vh_ref, vt_ref, vbuf)):
        hist, rows = src_h[...].astype(BF16), src_t[...].astype(BF16)
        for kh in range(N_KV_HEADS):
            m = (col_head == kh).astype(BF16)
            if chained:
                buf[kh, 0:WINDOW, :] = hist * m
                buf[kh, WINDOW:, :] = rows * m
            else:
                for s in range(nunits):
                    buf[kh, s * win:s * win + WINDOW, :] = hist[s * WINDOW:(s + 1) * WINDOW, :] * m
                    buf[kh, s * win + WINDOW:(s + 1) * win, :] = rows[s * chunk:(s + 1) * chunk, :] * m

    zero_keys = jnp.zeros((pad, D_KV), BF16)
    key_pos = lax.broadcasted_iota(jnp.int32, (1, MXU_TILE), 1)
    for c in range(nunits):
        r0 = c * chunk
        k0 = r0 if chained else c * win
        qst = jnp.concatenate(
            [q_ref[r0:r0 + chunk, g * D_KV:(g + 1) * D_KV] for g in range(GROUP)], axis=0)

        def keys(buf):
            return jnp.concatenate(
                [blk for kh in range(N_KV_HEADS) for blk in (zero_keys, buf[kh, k0:k0 + win, :])], axis=0)

        s_all = lax.dot_general(qst, keys(kbuf), (((1,), (1,)), ((), ())), preferred_element_type=F32)
        valid = key_pos >= pad
        if chained and r0 < WINDOW:
            valid = jnp.logical_and(valid, jnp.logical_or(key_pos >= pad + WINDOW - r0, ti > 0))
        probs = []
        for kh in range(N_KV_HEADS):
            s = jnp.where(valid, s_all[:, kh * MXU_TILE:(kh + 1) * MXU_TILE], fill_ref[kh])
            e = jnp.exp(s - jnp.max(s, axis=-1, keepdims=True))
            probs.append((e * (1.0 / jnp.sum(e, axis=-1, keepdims=True))).astype(BF16))
        o = _dot(jnp.concatenate(probs, axis=1), keys(vbuf))
        for g in range(GROUP):
            attn_s[r0:r0 + chunk, g * D_KV:(g + 1) * D_KV] = o[g * chunk:(g + 1) * chunk, :].astype(BF16)

    br = _dot(attn_s[...], wba_ref[...])
    mix = (z_ref[...] + g1_ref[...].astype(F32) * br).astype(BF16)
    h = x_ref[...] + _dot(mix, wout_ref[...])
    hn = (h * _rms_scale(h) * gffn_ref[...]).astype(BF16)

    taps = FFN_CONV_W - 1
    nstreams = 1 if chained else nunits
    seg = tq // nstreams
    pitch = seg + SUBLANES
    if chained:
        @pl.when(ti == 0)
        def _():
            gbuf[0:SUBLANES, :] = fs_ref[...]
    gate = _dot(hn, wup_ref[:, :D_FF])
    up = _dot(hn, wup_ref[:, D_FF:])
    parts = []
    for r in range(nstreams):
        base = r * pitch
        if not chained:
            gbuf[base:base + SUBLANES, :] = fs_ref[r]
        gbuf[base + SUBLANES:base + pitch, :] = gate[r * seg:(r + 1) * seg, :]
        gc = bfc_ref[...]
        for j in range(FFN_CONV_W):
            lo = base + SUBLANES - taps + j
            gc = gc + wfc_ref[j:j + 1, :] * gbuf[lo:lo + seg, :]
        parts.append(gc)
    gc = parts[0] if nstreams == 1 else jnp.concatenate(parts, axis=0)
    act = (jax.nn.gelu(gc) * up).astype(BF16)
    y_ref[...] = h + _dot(act, wdn_ref[...])

    if chained:
        gbuf[0:SUBLANES, :] = gbuf[tq:tq + SUBLANES, :]

        @pl.when(ti == pl.num_programs(1) - 1)
        def _():
            fso_ref[...] = gbuf[SUBLANES - taps:SUBLANES, :]
    else:
        for r in range(nstreams):
            fso_ref[r] = gbuf[(r + 1) * pitch - taps:(r + 1) * pitch, :]


def _attn_ffn(q, k, v, k_hist, v_hist, z, g1, x, ffn_state, lw, b, t_len, chunk, chained):
    taps = FFN_CONV_W - 1
    if chained:
        tq, ns = _pick(t_len, ROW_TILE), 1
        nt = t_len // tq
        grid = (b, nt)
        row = lambda w: pl.BlockSpec((tq, w), lambda bi, ti: (bi * nt + ti, 0))
        per_b, per_t = t_len // WINDOW, tq // WINDOW
        hist = pl.BlockSpec((WINDOW, D_KV), lambda bi, ti: (jnp.maximum(bi * per_b + ti * per_t - 1, 0), 0))
        kv_rows = WINDOW + tq
        state = lambda r: pl.BlockSpec((None, r, D_FF), lambda bi, ti: (bi, 0, 0))
    else:
        assert t_len == chunk
        ns = SUBLANES if b % SUBLANES == 0 else b
        tq = ns * chunk
        grid = (b // ns, 1)
        row = lambda w: pl.BlockSpec((tq, w), lambda bi, ti: (bi, 0))
        hist = pl.BlockSpec((ns * WINDOW, D_KV), lambda bi, ti: (bi, 0))
        kv_rows = ns * (WINDOW + chunk)
        state = lambda r: pl.BlockSpec((ns, r, D_FF), lambda bi, ti: (bi, 0, 0))
    kern = functools.partial(_attn_ffn_kernel, tq=tq, chunk=chunk, chained=chained)
    return pl.pallas_call(
        kern,
        grid=grid,
        in_specs=[row(D_MODEL), hist, hist, row(D_KV), row(D_KV),
                  _const_spec((N_KV_HEADS, GROUP * chunk, MXU_TILE)),
                  row(D_MODEL), row(D_MODEL), row(D_MODEL),
                  _const_spec((D_MODEL, D_MODEL)), _const_spec((D_MODEL, D_MODEL)), _const_spec((1, D_MODEL)),
                  state(SUBLANES), _const_spec((D_MODEL, 2 * D_FF)),
                  _const_spec((FFN_CONV_W, D_FF)), _const_spec((1, D_FF)), _const_spec((D_FF, D_MODEL))],
        out_specs=[row(D_MODEL), state(taps)],
        out_shape=[jax.ShapeDtypeStruct((b * t_len, D_MODEL), F32),
                   jax.ShapeDtypeStruct((b, taps, D_FF), F32)],
        scratch_shapes=[pltpu.VMEM((N_KV_HEADS, kv_rows, D_KV), BF16)] * 2
                       + [pltpu.VMEM((tq, D_MODEL), BF16),
                          pltpu.VMEM((ns * (tq // ns + SUBLANES), D_FF), F32)],
        compiler_params=pltpu.CompilerParams(vmem_limit_bytes=VMEM_LIMIT),
        name="attn_ffn",
    )(q, k_hist, v_hist, k, v, lw["sink_fill"][chunk], z, g1, x, lw["w_br_attn"], lw["w_out"], lw["g_ffn"],
      ffn_state, lw["w_up"], lw["w_fconv"], lw["b_fconv"], lw["w_down"])


def _block_diag_groups(w):
    per = MXU_TILE // LRU_BLOCK
    w4 = w.reshape(N_GATE_GROUPS, per, LRU_BLOCK, LRU_BLOCK)
    return jnp.einsum("jncd,nm->jncmd", w4, jnp.eye(per, dtype=w.dtype)).reshape(
        N_GATE_GROUPS, MXU_TILE, MXU_TILE)


def _prep_layer(l, p, chunks):
    w_in = p["w_in"][l]
    wq = w_in[:, 2048:3072].reshape(D_MODEL, N_KV_HEADS, GROUP, HEAD_DIM).transpose(0, 2, 1, 3)
    w_in_p = jnp.concatenate(
        [w_in[:, :2048], wq.reshape(D_MODEL, D_MODEL), w_in[:, 3584:], w_in[:, 3072:3584]], axis=1)
    w_ba = p["w_br_attn"][l].reshape(N_KV_HEADS, GROUP, HEAD_DIM, D_MODEL).transpose(1, 0, 2, 3)
    per = MXU_TILE // HEAD_DIM
    sink = p["attn_sink"][l].reshape(N_KV_HEADS, GROUP)
    row = lambda v: v.reshape(1, -1)
    return {
        "g_mix": row(p["g_mix"][l]),
        "w_in": w_in_p.astype(BF16),
        "b_gate": row(p["b_gate"][l]),
        "g_q": row(jnp.tile(p["g_q"][l], per)),
        "g_k": row(jnp.tile(p["g_k"][l], per)),
        "gn": (jnp.kron(jnp.eye(per, dtype=F32), jnp.ones((HEAD_DIM, HEAD_DIM), F32)) / HEAD_DIM).astype(BF16),
        "w_lconv": jnp.broadcast_to(p["w_lru_conv"][l][:, None, :], (LRU_CONV_W, SUBLANES, D_LRU)),
        "b_lconv": jnp.broadcast_to(p["b_lru_conv"][l][None, :], (SUBLANES, D_LRU)),
        "w_ai": jnp.concatenate([_block_diag_groups(p["w_lru_a"][l]), _block_diag_groups(p["w_lru_i"][l])],
                                axis=2).astype(BF16),
        "b_a": row(p["b_lru_a"][l]),
        "b_i": row(p["b_lru_i"][l]),
        "lam": row(p["lru_lambda"][l]),
        "w_br_lru": p["w_br_lru"][l].astype(BF16),
        "sink_fill": {c: jnp.full((N_KV_HEADS, GROUP * c, MXU_TILE), NEG, F32).at[:, :, 0].set(
            jnp.repeat(sink, c, axis=1)) for c in chunks},
        "w_br_attn": w_ba.reshape(D_MODEL, D_MODEL).astype(BF16),
        "w_out": p["w_out"][l].astype(BF16),
        "g_ffn": row(p["g_ffn"][l]),
        "w_up": p["w_up"][l].astype(BF16),
        "w_fconv": p["w_ffn_conv"][l],
        "b_fconv": row(p["b_ffn_conv"][l]),
        "w_down": p["w_down"][l].astype(BF16),
    }


def _pick(n, pref):
    return pref if n % pref == 0 else n


def _layer(x, lw, conv_state, h_state, ffn_state, k_hist, v_hist, b, t_len, chunk, chained):
    z, q, g1, k, v, conv_new, h_new = _proj_lru(x, conv_state, h_state, lw, b, t_len, chained)
    if chained:
        k_hist, v_hist = k, v
    y, ffn_new = _attn_ffn(q, k, v, k_hist, v_hist, z, g1, x, ffn_state, lw, b, t_len, chunk, chained)
    return y, k, v, conv_new, h_new, ffn_new


def _last_window(kv, b, t_len):
    return kv.reshape(b, t_len, D_KV)[:, t_len - WINDOW:].reshape(b, WINDOW, N_KV_HEADS, HEAD_DIM)


def _front_pad(state):
    return jnp.pad(state, ((0, 0), (SUBLANES - state.shape[1], 0), (0, 0)))


def kernel(x_prompt, x_sample, state_lru_conv, state_lru_h, cache_k, cache_v, state_ffn_conv, g_mix, w_in, b_gate, w_lru_conv, b_lru_conv, w_lru_a, b_lru_a, w_lru_i, b_lru_i, lru_lambda, g_q, g_k, attn_sink, w_br_lru, w_br_attn, w_out, g_ffn, w_up, w_ffn_conv, b_ffn_conv, w_down):
    params = dict(g_mix=g_mix, w_in=w_in, b_gate=b_gate, w_lru_conv=w_lru_conv, b_lru_conv=b_lru_conv,
                  w_lru_a=w_lru_a, b_lru_a=b_lru_a, w_lru_i=w_lru_i, b_lru_i=b_lru_i, lru_lambda=lru_lambda,
                  g_q=g_q, g_k=g_k, attn_sink=attn_sink, w_br_lru=w_br_lru, w_br_attn=w_br_attn, w_out=w_out,
                  g_ffn=g_ffn, w_up=w_up, w_ffn_conv=w_ffn_conv, b_ffn_conv=b_ffn_conv, w_down=w_down)
    depth = w_in.shape[0]
    bp, sp, _ = x_prompt.shape
    bs, ss, _ = x_sample.shape
    p_chunk, s_chunk = PROMPT_CHUNK, ss
    xp = x_prompt.reshape(bp * sp, D_MODEL)
    xs = x_sample.reshape(bs * ss, D_MODEL)
    zeros_p = (jnp.zeros((bp, SUBLANES, D_LRU), F32), jnp.zeros((bp, 1, D_LRU), F32),
               jnp.zeros((bp, SUBLANES, D_FF), F32))
    outs = {n: [] for n in ("p_lc", "p_lh", "p_k", "p_v", "p_fc", "s_lc", "s_lh", "s_k", "s_v", "s_fc")}
    for l in range(depth):
        lw = _prep_layer(l, params, {p_chunk, s_chunk})
        xp, k, v, lc, lh, fc = _layer(xp, lw, *zeros_p, None, None, bp, sp, p_chunk, True)
        outs["p_lc"].append(lc)
        outs["p_lh"].append(lh.reshape(bp, D_LRU))
        outs["p_k"].append(_last_window(k, bp, sp))
        outs["p_v"].append(_last_window(v, bp, sp))
        outs["p_fc"].append(fc)
        xs, k, v, lc, lh, fc = _layer(
            xs, lw, _front_pad(state_lru_conv[l]), state_lru_h[l], _front_pad(state_ffn_conv[l]),
            cache_k[l].reshape(bs * WINDOW, D_KV), cache_v[l].reshape(bs * WINDOW, D_KV),
            bs, ss, s_chunk, False)
        outs["s_lc"].append(lc.transpose(1, 0, 2))
        outs["s_lh"].append(lh.reshape(bs, D_LRU))
        outs["s_k"].append(k.reshape(bs, ss, N_KV_HEADS, HEAD_DIM))
        outs["s_v"].append(v.reshape(bs, ss, N_KV_HEADS, HEAD_DIM))
        outs["s_fc"].append(fc)
    st = {n: jnp.stack(v) for n, v in outs.items()}
    return (xp.reshape(bp, sp, D_MODEL), xs.reshape(bs, ss, D_MODEL),
            st["p_lc"], st["p_lh"], st["p_k"], st["p_v"], st["p_fc"],
            st["s_lc"], st["s_lh"], st["s_k"], st["s_v"], st["s_fc"])
```

```python
import functools

import jax
import jax.numpy as jnp
from jax import lax
from jax.experimental import pallas as pl
from jax.experimental.pallas import tpu as pltpu

D_MODEL = 1024
N_HEADS = 16
N_KV_HEADS = 4
HEAD_DIM = 64
GROUP = N_HEADS // N_KV_HEADS
D_KV = N_KV_HEADS * HEAD_DIM
WINDOW = 128
D_LRU = D_MODEL
N_LRU_BLOCKS = 16
LRU_BLOCK = D_LRU // N_LRU_BLOCKS
LRU_C = 8.0
LRU_CONV_W = 4
D_FF = 3 * D_MODEL
FFN_CONV_W = 3
EPS = 1e-6
NEG = -1e30

MXU_TILE = 256
SUBLANES = 8
LANES = 128
SEG = 32
ROW_TILE = 256
PROMPT_CHUNK = 64
SEG_PITCH = SEG + SUBLANES
TINY = 1e-30
N_GATE_GROUPS = D_LRU // MXU_TILE

_Q0, _K0, _G0 = 2 * D_LRU, 2 * D_LRU + D_MODEL, 2 * D_LRU + D_MODEL + 2 * D_KV

VMEM_LIMIT = 56 * 1024 * 1024

BF16 = jnp.bfloat16
F32 = jnp.float32


def _dot(a, b):
    return jnp.dot(a, b, preferred_element_type=F32)


def _rms_scale(x):
    return lax.rsqrt(jnp.mean(x * x, axis=-1, keepdims=True) + EPS)


def _const_spec(shape):
    nd = len(shape)
    return pl.BlockSpec(shape, lambda *_: (0,) * nd, pipeline_mode=pl.Buffered(1))


def _head_norm(t, gain, gn):
    ms = _dot((t * t).astype(BF16), gn)
    return t * lax.rsqrt(ms + EPS) * gain


def _lru_body(u, gy_fn, cs_ref, h0_ref, wc_ref, bc_ref, wai_ref, ba_ref, bi_ref, lam_ref, wbr_ref,
              us, hsl, carry_u, carry_h, *, nseg, chained):
    nlb = D_LRU // LANES
    taps = LRU_CONV_W - 1

    for lb in range(nlb):
        cols = slice(lb * LANES, (lb + 1) * LANES)
        for r in range(nseg):
            base = r * SEG_PITCH
            rows = slice(r * SEG, (r + 1) * SEG)
            if not chained:
                prev = cs_ref[r, :, cols]
            elif r == 0:
                prev = carry_u[:, cols]
            else:
                prev = u[r * SEG - SUBLANES:r * SEG, cols]
            us[lb, base:base + SUBLANES, :] = prev
            us[lb, base + SUBLANES:base + SEG_PITCH, :] = u[rows, cols]

    def seg_rows(i):
        return pl.ds(SUBLANES + i, nseg, stride=SEG_PITCH)

    xc_lb, u_tail = [], []
    for lb in range(nlb):
        cols = slice(lb * LANES, (lb + 1) * LANES)
        w = [wc_ref[k, 0:nseg, cols] for k in range(LRU_CONV_W)]
        bias = bc_ref[0:nseg, cols]
        ut = [us[lb, seg_rows(i - taps), :] for i in range(SEG + taps)]
        steps = []
        for j in range(SEG):
            x = bias
            for k in range(LRU_CONV_W):
                x = x + w[k] * ut[j + k]
            steps.append(x)
        xc_lb.append(jnp.concatenate(steps, axis=0))
        u_tail.append(ut[SEG:])
    xc = jnp.concatenate(xc_lb, axis=1)

    lam = lam_ref[...]
    softplus_neg_lam = jnp.maximum(-lam, 0.0) + jnp.log1p(jnp.exp(-jnp.abs(lam)))
    log_a_per_r = -LRU_C * softplus_neg_lam
    xcb = xc.astype(BF16)
    a_lb, b_lb = [], []
    for j in range(N_GATE_GROUPS):
        cols = slice(j * MXU_TILE, (j + 1) * MXU_TILE)
        ri = _dot(xcb[:, cols], wai_ref[j])
        r = jax.nn.sigmoid(ri[:, :MXU_TILE] + ba_ref[:, cols])
        i = jax.nn.sigmoid(ri[:, MXU_TILE:] + bi_ref[:, cols])
        log_a = r * log_a_per_r[:, cols]
        a = jnp.exp(log_a)
        one_minus_a2 = -jnp.tanh(log_a) * (a * a + 1.0)
        mult = one_minus_a2 * lax.rsqrt(jnp.maximum(one_minus_a2, TINY))
        b = mult * i * xc[:, cols]
        for k in range(MXU_TILE // LANES):
            a_lb.append(a[:, k * LANES:(k + 1) * LANES])
            b_lb.append(b[:, k * LANES:(k + 1) * LANES])

    h_last, c_next = [], []
    for lb in range(nlb):
        cols = slice(lb * LANES, (lb + 1) * LANES)
        step = lambda v, j: v[j * nseg:(j + 1) * nseg, :]
        if chained:
            hs, ap = jnp.zeros((nseg, LANES), F32), jnp.ones((nseg, LANES), F32)
            hs_l, ap_l = [], []
            for j in range(SEG):
                a = step(a_lb[lb], j)
                hs = a * hs + step(b_lb[lb], j)
                ap = a * ap
                hs_l.append(hs)
                ap_l.append(ap)
            c = carry_h[:, cols]
            cin = []
            for r in range(nseg):
                cin.append(c)
                c = ap[r:r + 1, :] * c + hs[r:r + 1, :]
            c_next.append(c)
            cin = jnp.concatenate(cin, axis=0)
            h_l = [hs_l[j] + ap_l[j] * cin for j in range(SEG)]
        else:
            h, h_l = h0_ref[:, cols], []
            for j in range(SEG):
                h = step(a_lb[lb], j) * h + step(b_lb[lb], j)
                h_l.append(h)
            h_last.append(h)
        for j in range(SEG):
            hsl[lb, seg_rows(j), :] = h_l[j]

    h_nat = jnp.concatenate(
        [jnp.concatenate([hsl[lb, r * SEG_PITCH + SUBLANES:(r + 1) * SEG_PITCH, :] for r in range(nseg)], axis=0)
         for lb in range(nlb)], axis=1)
    br_nat = _dot((h_nat * gy_fn()).astype(BF16), wbr_ref[...])

    if chained:
        return br_nat, None, jnp.concatenate(c_next, axis=1)
    conv_tail = [jnp.concatenate([u_tail[lb][k] for lb in range(nlb)], axis=1) for k in range(taps)]
    return br_nat, conv_tail, jnp.concatenate(h_last, axis=1)


def _proj_lru_kernel(x_ref, gmix_ref, wuy_ref, wq_ref, wg_ref, wkv_ref, bg_ref, gq_ref, gk_ref, gn_ref,
                     cs_ref, h0_ref, wc_ref, bc_ref, wai_ref, ba_ref, bi_ref, lam_ref, wbr_ref,
                     z_ref, q_ref, g1_ref, k_ref, v_ref, cso_ref, ho_ref,
                     us, hsl, carry_u, carry_h, *, nseg, chained):
    t = pl.program_id(1)
    tm = nseg * SEG
    taps = LRU_CONV_W - 1
    if chained:
        @pl.when(t == 0)
        def _():
            carry_u[...] = cs_ref[...]
            carry_h[...] = h0_ref[...]

    x = x_ref[...]
    xn = (x * _rms_scale(x) * gmix_ref[...]).astype(BF16)

    u = _dot(xn, wuy_ref[:, :D_LRU])
    br, conv_tail, h_end = _lru_body(
        u, lambda: jax.nn.gelu(_dot(xn, wuy_ref[:, D_LRU:])), cs_ref, h0_ref, wc_ref, bc_ref, wai_ref, ba_ref, bi_ref, lam_ref, wbr_ref,
        us, hsl, carry_u, carry_h, nseg=nseg, chained=chained)
    gates = jax.nn.sigmoid(_dot(xn, wg_ref[...]) + bg_ref[...])
    z_ref[...] = gates[:, :D_MODEL] * br
    g1_ref[...] = gates[:, D_MODEL:].astype(BF16)

    q = _dot(xn, wq_ref[...])
    kv = _dot(xn, wkv_ref[...])
    scale = HEAD_DIM ** -0.5
    for j in range(D_MODEL // MXU_TILE):
        cols = slice(j * MXU_TILE, (j + 1) * MXU_TILE)
        q_ref[:, cols] = (_head_norm(q[:, cols], gq_ref[...], gn_ref[...]) * scale).astype(BF16)
    k_ref[...] = _head_norm(kv[:, :D_KV], gk_ref[...], gn_ref[...])
    v_ref[...] = kv[:, D_KV:]

    if chained:
        carry_u[...] = u[tm - SUBLANES:tm, :]
        carry_h[...] = h_end

        @pl.when(t == pl.num_programs(1) - 1)
        def _():
            cso_ref[...] = u[tm - taps:tm, :]
            ho_ref[...] = h_end
    else:
        for k in range(taps):
            cso_ref[k] = conv_tail[k]
        ho_ref[...] = h_end


def _proj_lru(x, conv_state, h_state, lw, b, t_len, chained):
    taps = LRU_CONV_W - 1
    if chained:
        nseg = SUBLANES
        tm = nseg * SEG
        nt = t_len // tm
        grid = (b, nt)
        row = lambda w: pl.BlockSpec((tm, w), lambda bi, ti: (bi * nt + ti, 0))
        per_b = lambda r: pl.BlockSpec((None, r, D_LRU), lambda bi, ti: (bi, 0, 0))
        state_specs = [per_b(SUBLANES), per_b(1)]
        out_state_specs = [per_b(taps), per_b(1)]
        out_state_shapes = [jax.ShapeDtypeStruct((b, taps, D_LRU), F32), jax.ShapeDtypeStruct((b, 1, D_LRU), F32)]
    else:
        assert t_len == SEG
        nseg = SUBLANES if b % SUBLANES == 0 else b
        tm = nseg * SEG
        grid = (b // nseg, 1)
        row = lambda w: pl.BlockSpec((tm, w), lambda bi, ti: (bi, 0))
        state_specs = [pl.BlockSpec((nseg, SUBLANES, D_LRU), lambda bi, ti: (bi, 0, 0)),
                       pl.BlockSpec((nseg, D_LRU), lambda bi, ti: (bi, 0))]
        out_state_specs = [pl.BlockSpec((taps, nseg, D_LRU), lambda bi, ti: (0, bi, 0)),
                           pl.BlockSpec((nseg, D_LRU), lambda bi, ti: (bi, 0))]
        out_state_shapes = [jax.ShapeDtypeStruct((taps, b, D_LRU), F32), jax.ShapeDtypeStruct((b, D_LRU), F32)]
    m = b * t_len
    vec = _const_spec((1, D_LRU))
    slab = pltpu.VMEM((D_LRU // LANES, nseg * SEG_PITCH, LANES), F32)
    kern = functools.partial(_proj_lru_kernel, nseg=nseg, chained=chained)
    return pl.pallas_call(
        kern,
        grid=grid,
        in_specs=[row(D_MODEL), _const_spec((1, D_MODEL)), _const_spec((D_MODEL, 2 * D_LRU)),
                  _const_spec((D_MODEL, D_MODEL)), _const_spec((D_MODEL, 2 * D_MODEL)),
                  _const_spec((D_MODEL, 2 * D_KV)), _const_spec((1, 2 * D_MODEL)), _const_spec((1, MXU_TILE)), _const_spec((1, MXU_TILE)),
                  _const_spec((MXU_TILE, MXU_TILE))] + state_specs
                 + [_const_spec((LRU_CONV_W, SUBLANES, D_LRU)), _const_spec((SUBLANES, D_LRU)),
                    _const_spec((N_GATE_GROUPS, MXU_TILE, 2 * MXU_TILE)), vec, vec, vec,
                    _const_spec((D_LRU, D_MODEL))],
        out_specs=[row(D_MODEL), row(D_MODEL), row(D_MODEL), row(D_KV), row(D_KV)] + out_state_specs,
        out_shape=[jax.ShapeDtypeStruct((m, D_MODEL), F32), jax.ShapeDtypeStruct((m, D_MODEL), BF16),
                   jax.ShapeDtypeStruct((m, D_MODEL), BF16), jax.ShapeDtypeStruct((m, D_KV), F32),
                   jax.ShapeDtypeStruct((m, D_KV), F32)] + out_state_shapes,
        scratch_shapes=[slab, slab, pltpu.VMEM((SUBLANES, D_LRU), F32), pltpu.VMEM((1, D_LRU), F32)],
        compiler_params=pltpu.CompilerParams(vmem_limit_bytes=VMEM_LIMIT),
        name="proj_lru",
    )(x, lw["g_mix"], lw["w_uy"], lw["w_q"], lw["w_gates"], lw["w_kv"], lw["b_gate"], lw["g_q"], lw["g_k"], lw["gn"], conv_state, h_state,
      lw["w_lconv"], lw["b_lconv"], lw["w_ai"], lw["b_a"], lw["b_i"], lw["lam"], lw["w_br_lru"])


def _attn_ffn_kernel(q_ref, kh_ref, vh_ref, kt_ref, vt_ref, fill_ref, z_ref, g1_ref, x_ref,
                     wba_ref, wout_ref, gffn_ref, fs_ref, wup_ref, wfc_ref, bfc_ref, wdn_ref,
                     y_ref, fso_ref, kbuf, vbuf, attn_s, gbuf, *, tq, chunk, chained):
    ti = pl.program_id(1)
    nunits = tq // chunk
    win = WINDOW + chunk
    pad = MXU_TILE - win
    col_head = lax.broadcasted_iota(jnp.int32, (1, D_KV), 1) // HEAD_DIM
    for src_h, src_t, buf in ((kh_ref, kt_ref, kbuf), (vh_ref, vt_ref, vbuf)):
        hist, rows = src_h[...].astype(BF16), src_t[...].astype(BF16)
        for kh in range(N_KV_HEADS):
            m = (col_head == kh).astype(BF16)
            if chained:
                buf[kh, 0:WINDOW, :] = hist * m
                buf[kh, WINDOW:, :] = rows * m
            else:
                for s in range(nunits):
                    buf[kh, s * win:s * win + WINDOW, :] = hist[s * WINDOW:(s + 1) * WINDOW, :] * m
                    buf[kh, s * win + WINDOW:(s + 1) * win, :] = rows[s * chunk:(s + 1) * chunk, :] * m

    zero_keys = jnp.zeros((pad, D_KV), BF16)
    key_pos = lax.broadcasted_iota(jnp.int32, (1, MXU_TILE), 1)
    for c in range(nunits):
        r0 = c * chunk
        k0 = r0 if chained else c * win
        qst = jnp.concatenate(
            [q_ref[r0:r0 + chunk, g * D_KV:(g + 1) * D_KV] for g in range(GROUP)], axis=0)

        def keys(buf):
            return jnp.concatenate(
                [blk for kh in range(N_KV_HEADS) for blk in (zero_keys, buf[kh, k0:k0 + win, :])], axis=0)

        s_all = lax.dot_general(qst, keys(kbuf), (((1,), (1,)), ((), ())), preferred_element_type=F32)
        valid = key_pos >= pad
        if chained and r0 < WINDOW:
            valid = jnp.logical_and(valid, jnp.logical_or(key_pos >= pad + WINDOW - r0, ti > 0))
        probs = []
        for kh in range(N_KV_HEADS):
            s = jnp.where(valid, s_all[:, kh * MXU_TILE:(kh + 1) * MXU_TILE], fill_ref[kh])
            e = jnp.exp(s - jnp.max(s, axis=-1, keepdims=True))
            probs.append((e * (1.0 / jnp.sum(e, axis=-1, keepdims=True))).astype(BF16))
        o = _dot(jnp.concatenate(probs, axis=1), keys(vbuf))
        for g in range(GROUP):
            attn_s[r0:r0 + chunk, g * D_KV:(g + 1) * D_KV] = o[g * chunk:(g + 1) * chunk, :].astype(BF16)

    br = _dot(attn_s[...], wba_ref[...])
    mix = (z_ref[...] + g1_ref[...].astype(F32) * br).astype(BF16)
    h = x_ref[...] + _dot(mix, wout_ref[...])
    hn = (h * _rms_scale(h) * gffn_ref[...]).astype(BF16)

    taps = FFN_CONV_W - 1
    nstreams = 1 if chained else nunits
    seg = tq // nstreams
    pitch = seg + SUBLANES
    if chained:
        @pl.when(ti == 0)
        def _():
            gbuf[0:SUBLANES, :] = fs_ref[...]
    gate = _dot(hn, wup_ref[:, :D_FF])
    up = _dot(hn, wup_ref[:, D_FF:])
    parts = []
    for r in range(nstreams):
        base = r * pitch
        if not chained:
            gbuf[base:base + SUBLANES, :] = fs_ref[r]
        gbuf[base + SUBLANES:base + pitch, :] = gate[r * seg:(r + 1) * seg, :]
        gc = bfc_ref[...]
        for j in range(FFN_CONV_W):
            lo = base + SUBLANES - taps + j
            gc = gc + wfc_ref[j:j + 1, :] * gbuf[lo:lo + seg, :]
        parts.append(gc)
    gc = parts[0] if nstreams == 1 else jnp.concatenate(parts, axis=0)
    act = (jax.nn.gelu(gc) * up).astype(BF16)
    y_ref[...] = h + _dot(act, wdn_ref[...])

    if chained:
        gbuf[0:SUBLANES, :] = gbuf[tq:tq + SUBLANES, :]

        @pl.when(ti == pl.num_programs(1) - 1)
        def _():
            fso_ref[...] = gbuf[SUBLANES - taps:SUBLANES, :]
    else:
        for r in range(nstreams):
            fso_ref[r] = gbuf[(r + 1) * pitch - taps:(r + 1) * pitch, :]


def _attn_ffn(q, k, v, k_hist, v_hist, z, g1, x, ffn_state, lw, b, t_len, chunk, chained):
    taps = FFN_CONV_W - 1
    if chained:
        tq, ns = _pick(t_len, ROW_TILE), 1
        nt = t_len // tq
        grid = (b, nt)
        row = lambda w: pl.BlockSpec((tq, w), lambda bi, ti: (bi * nt + ti, 0))
        per_b, per_t = t_len // WINDOW, tq // WINDOW
        hist = pl.BlockSpec((WINDOW, D_KV), lambda bi, ti: (jnp.maximum(bi * per_b + ti * per_t - 1, 0), 0))
        kv_rows = WINDOW + tq
        state = lambda r: pl.BlockSpec((None, r, D_FF), lambda bi, ti: (bi, 0, 0))
    else:
        assert t_len == chunk
        ns = SUBLANES if b % SUBLANES == 0 else b
        tq = ns * chunk
        grid = (b // ns, 1)
        row = lambda w: pl.BlockSpec((tq, w), lambda bi, ti: (bi, 0))
        hist = pl.BlockSpec((ns * WINDOW, D_KV), lambda bi, ti: (bi, 0))
        kv_rows = ns * (WINDOW + chunk)
        state = lambda r: pl.BlockSpec((ns, r, D_FF), lambda bi, ti: (bi, 0, 0))
    kern = functools.partial(_attn_ffn_kernel, tq=tq, chunk=chunk, chained=chained)
    return pl.pallas_call(
        kern,
        grid=grid,
        in_specs=[row(D_MODEL), hist, hist, row(D_KV), row(D_KV),
                  _const_spec((N_KV_HEADS, GROUP * chunk, MXU_TILE)),
                  row(D_MODEL), row(D_MODEL), row(D_MODEL),
                  _const_spec((D_MODEL, D_MODEL)), _const_spec((D_MODEL, D_MODEL)), _const_spec((1, D_MODEL)),
                  state(SUBLANES), _const_spec((D_MODEL, 2 * D_FF)),
                  _const_spec((FFN_CONV_W, D_FF)), _const_spec((1, D_FF)), _const_spec((D_FF, D_MODEL))],
        out_specs=[row(D_MODEL), state(taps)],
        out_shape=[jax.ShapeDtypeStruct((b * t_len, D_MODEL), F32),
                   jax.ShapeDtypeStruct((b, taps, D_FF), F32)],
        scratch_shapes=[pltpu.VMEM((N_KV_HEADS, kv_rows, D_KV), BF16)] * 2
                       + [pltpu.VMEM((tq, D_MODEL), BF16),
                          pltpu.VMEM((ns * (tq // ns + SUBLANES), D_FF), F32)],
        compiler_params=pltpu.CompilerParams(vmem_limit_bytes=VMEM_LIMIT),
        name="attn_ffn",
    )(q, k_hist, v_hist, k, v, lw["sink_fill"][chunk], z, g1, x, lw["w_br_attn"], lw["w_out"], lw["g_ffn"],
      ffn_state, lw["w_up"], lw["w_fconv"], lw["b_fconv"], lw["w_down"])


def _block_diag_groups(w):
    per = MXU_TILE // LRU_BLOCK
    w4 = w.reshape(N_GATE_GROUPS, per, LRU_BLOCK, LRU_BLOCK)
    return jnp.einsum("jncd,nm->jncmd", w4, jnp.eye(per, dtype=w.dtype)).reshape(
        N_GATE_GROUPS, MXU_TILE, MXU_TILE)


def _prep_layer(l, p, chunks):
    w_in = p["w_in"][l]
    wq = w_in[:, _Q0:_K0].reshape(D_MODEL, N_KV_HEADS, GROUP, HEAD_DIM).transpose(0, 2, 1, 3)
    w_ba = p["w_br_attn"][l].reshape(N_KV_HEADS, GROUP, HEAD_DIM, D_MODEL).transpose(1, 0, 2, 3)
    per = MXU_TILE // HEAD_DIM
    sink = p["attn_sink"][l].reshape(N_KV_HEADS, GROUP)
    row = lambda v: v.reshape(1, -1)
    return {
        "g_mix": row(p["g_mix"][l]),
        "w_uy": w_in[:, :_Q0].astype(BF16),
        "w_q": wq.reshape(D_MODEL, D_MODEL).astype(BF16),
        "w_gates": w_in[:, _G0:].astype(BF16),
        "w_kv": w_in[:, _K0:_G0].astype(BF16),
        "b_gate": row(p["b_gate"][l]),
        "g_q": row(jnp.tile(p["g_q"][l], per)),
        "g_k": row(jnp.tile(p["g_k"][l], per)),
        "gn": (jnp.kron(jnp.eye(per, dtype=F32), jnp.ones((HEAD_DIM, HEAD_DIM), F32)) / HEAD_DIM).astype(BF16),
        "w_lconv": jnp.broadcast_to(p["w_lru_conv"][l][:, None, :], (LRU_CONV_W, SUBLANES, D_LRU)),
        "b_lconv": jnp.broadcast_to(p["b_lru_conv"][l][None, :], (SUBLANES, D_LRU)),
        "w_ai": jnp.concatenate([_block_diag_groups(p["w_lru_a"][l]), _block_diag_groups(p["w_lru_i"][l])],
                                axis=2).astype(BF16),
        "b_a": row(p["b_lru_a"][l]),
        "b_i": row(p["b_lru_i"][l]),
        "lam": row(p["lru_lambda"][l]),
        "w_br_lru": p["w_br_lru"][l].astype(BF16),
        "sink_fill": {c: jnp.full((N_KV_HEADS, GROUP * c, MXU_TILE), NEG, F32).at[:, :, 0].set(
            jnp.repeat(sink, c, axis=1)) for c in chunks},
        "w_br_attn": w_ba.reshape(D_MODEL, D_MODEL).astype(BF16),
        "w_out": p["w_out"][l].astype(BF16),
        "g_ffn": row(p["g_ffn"][l]),
        "w_up": p["w_up"][l].astype(BF16),
        "w_fconv": p["w_ffn_conv"][l],
        "b_fconv": row(p["b_ffn_conv"][l]),
        "w_down": p["w_down"][l].astype(BF16),
    }


def _pick(n, pref):
    return pref if n % pref == 0 else n


def _layer(x, lw, conv_state, h_state, ffn_state, k_hist, v_hist, b, t_len, chunk, chained):
    z, q, g1, k, v, conv_new, h_new = _proj_lru(x, conv_state, h_state, lw, b, t_len, chained)
    if chained:
        k_hist, v_hist = k, v
    y, ffn_new = _attn_ffn(q, k, v, k_hist, v_hist, z, g1, x, ffn_state, lw, b, t_len, chunk, chained)
    return y, k, v, conv_new, h_new, ffn_new


def _last_window(kv, b, t_len):
    return kv.reshape(b, t_len, D_KV)[:, t_len - WINDOW:].reshape(b, WINDOW, N_KV_HEADS, HEAD_DIM)


def _front_pad(state):
    return jnp.pad(state, ((0, 0), (SUBLANES - state.shape[1], 0), (0, 0)))


def kernel(x_prompt, x_sample, state_lru_conv, state_lru_h, cache_k, cache_v, state_ffn_conv, g_mix, w_in, b_gate, w_lru_conv, b_lru_conv, w_lru_a, b_lru_a, w_lru_i, b_lru_i, lru_lambda, g_q, g_k, attn_sink, w_br_lru, w_br_attn, w_out, g_ffn, w_up, w_ffn_conv, b_ffn_conv, w_down):
    params = dict(g_mix=g_mix, w_in=w_in, b_gate=b_gate, w_lru_conv=w_lru_conv, b_lru_conv=b_lru_conv,
                  w_lru_a=w_lru_a, b_lru_a=b_lru_a, w_lru_i=w_lru_i, b_lru_i=b_lru_i, lru_lambda=lru_lambda,
                  g_q=g_q, g_k=g_k, attn_sink=attn_sink, w_br_lru=w_br_lru, w_br_attn=w_br_attn, w_out=w_out,
                  g_ffn=g_ffn, w_up=w_up, w_ffn_conv=w_ffn_conv, b_ffn_conv=b_ffn_conv, w_down=w_down)
    depth = w_in.shape[0]
    bp, sp, _ = x_prompt.shape
    bs, ss, _ = x_sample.shape
    p_chunk, s_chunk = PROMPT_CHUNK, ss
    xp = x_prompt.reshape(bp * sp, D_MODEL)
    xs = x_sample.reshape(bs * ss, D_MODEL)
    zeros_p = (jnp.zeros((bp, SUBLANES, D_LRU), F32), jnp.zeros((bp, 1, D_LRU), F32),
               jnp.zeros((bp, SUBLANES, D_FF), F32))
    outs = {n: [] for n in ("p_lc", "p_lh", "p_k", "p_v", "p_fc", "s_lc", "s_lh", "s_k", "s_v", "s_fc")}
    for l in range(depth):
        lw = _prep_layer(l, params, {p_chunk, s_chunk})
        xp, k, v, lc, lh, fc = _layer(xp, lw, *zeros_p, None, None, bp, sp, p_chunk, True)
        outs["p_lc"].append(lc)
        outs["p_lh"].append(lh.reshape(bp, D_LRU))
        outs["p_k"].append(_last_window(k, bp, sp))
        outs["p_v"].append(_last_window(v, bp, sp))
        outs["p_fc"].append(fc)
        xs, k, v, lc, lh, fc = _layer(
            xs, lw, _front_pad(state_lru_conv[l]), state_lru_h[l], _front_pad(state_ffn_conv[l]),
            cache_k[l].reshape(bs * WINDOW, D_KV), cache_v[l].reshape(bs * WINDOW, D_KV),
            bs, ss, s_chunk, False)
        outs["s_lc"].append(lc.transpose(1, 0, 2))
        outs["s_lh"].append(lh.reshape(bs, D_LRU))
        outs["s_k"].append(k.reshape(bs, ss, N_KV_HEADS, HEAD_DIM))
        outs["s_v"].append(v.reshape(bs, ss, N_KV_HEADS, HEAD_DIM))
        outs["s_fc"].append(fc)
    st = {n: jnp.stack(v) for n, v in outs.items()}
    return (xp.reshape(bp, sp, D_MODEL), xs.reshape(bs, ss, D_MODEL),
            st["p_lc"], st["p_lh"], st["p_k"], st["p_v"], st["p_fc"],
            st["s_lc"], st["s_lh"], st["s_k"], st["s_v"], st["s_fc"])
```

```python
import functools

import jax
import jax.numpy as jnp
from jax import lax
from jax.experimental import pallas as pl
from jax.experimental.pallas import tpu as pltpu

D_MODEL = 1024
N_HEADS = 16
N_KV_HEADS = 4
HEAD_DIM = 64
GROUP = N_HEADS // N_KV_HEADS
D_KV = N_KV_HEADS * HEAD_DIM
WINDOW = 128
D_LRU = D_MODEL
N_LRU_BLOCKS = 16
LRU_BLOCK = D_LRU // N_LRU_BLOCKS
LRU_C = 8.0
LRU_CONV_W = 4
D_FF = 3 * D_MODEL
FFN_CONV_W = 3
EPS = 1e-6
NEG = -1e30

MXU_TILE = 256
SUBLANES = 8
LANES = 128
SEG = 32
ROW_TILE = 256
PROMPT_CHUNK = 64
SEG_PITCH = SEG + SUBLANES
TINY = 1e-30
N_GATE_GROUPS = D_LRU // MXU_TILE

_Q0, _K0, _G0 = 2 * D_LRU, 2 * D_LRU + D_MODEL, 2 * D_LRU + D_MODEL + 2 * D_KV

VMEM_LIMIT = 56 * 1024 * 1024

BF16 = jnp.bfloat16
F32 = jnp.float32


def _dot(a, b):
    return jnp.dot(a, b, preferred_element_type=F32)


def _rms_scale(x):
    return lax.rsqrt(jnp.mean(x * x, axis=-1, keepdims=True) + EPS)


def _const_spec(shape):
    nd = len(shape)
    return pl.BlockSpec(shape, lambda *_: (0,) * nd, pipeline_mode=pl.Buffered(1))


def _head_norm(t, gain, gn):
    ms = _dot((t * t).astype(BF16), gn)
    return t * lax.rsqrt(ms + EPS) * gain


def _lru_body(u, gy, cs_ref, h0_ref, wc_ref, bc_ref, wai_ref, ba_ref, bi_ref, lam_ref, wbr_ref,
              us, hsl, carry_u, carry_h, *, nseg, chained):
    nlb = D_LRU // LANES
    taps = LRU_CONV_W - 1

    for lb in range(nlb):
        cols = slice(lb * LANES, (lb + 1) * LANES)
        for r in range(nseg):
            base = r * SEG_PITCH
            rows = slice(r * SEG, (r + 1) * SEG)
            if not chained:
                prev = cs_ref[r, :, cols]
            elif r == 0:
                prev = carry_u[:, cols]
            else:
                prev = u[r * SEG - SUBLANES:r * SEG, cols]
            us[lb, base:base + SUBLANES, :] = prev
            us[lb, base + SUBLANES:base + SEG_PITCH, :] = u[rows, cols]

    def seg_rows(i):
        return pl.ds(SUBLANES + i, nseg, stride=SEG_PITCH)

    xc_lb, u_tail = [], []
    for lb in range(nlb):
        cols = slice(lb * LANES, (lb + 1) * LANES)
        w = [wc_ref[k, 0:nseg, cols] for k in range(LRU_CONV_W)]
        bias = bc_ref[0:nseg, cols]
        ut = [us[lb, seg_rows(i - taps), :] for i in range(SEG + taps)]
        steps = []
        for j in range(SEG):
            x = bias
            for k in range(LRU_CONV_W):
                x = x + w[k] * ut[j + k]
            steps.append(x)
        xc_lb.append(jnp.concatenate(steps, axis=0))
        u_tail.append(ut[SEG:])
    xc = jnp.concatenate(xc_lb, axis=1)

    lam = lam_ref[...]
    softplus_neg_lam = jnp.maximum(-lam, 0.0) + jnp.log1p(jnp.exp(-jnp.abs(lam)))
    log_a_per_r = -LRU_C * softplus_neg_lam
    xcb = xc.astype(BF16)
    a_lb, b_lb = [], []
    for j in range(N_GATE_GROUPS):
        cols = slice(j * MXU_TILE, (j + 1) * MXU_TILE)
        ri = _dot(xcb[:, cols], wai_ref[j])
        r = jax.nn.sigmoid(ri[:, :MXU_TILE] + ba_ref[:, cols])
        i = jax.nn.sigmoid(ri[:, MXU_TILE:] + bi_ref[:, cols])
        log_a = r * log_a_per_r[:, cols]
        a = jnp.exp(log_a)
        one_minus_a2 = -jnp.tanh(log_a) * (a * a + 1.0)
        mult = one_minus_a2 * lax.rsqrt(jnp.maximum(one_minus_a2, TINY))
        b = mult * i * xc[:, cols]
        for k in range(MXU_TILE // LANES):
            a_lb.append(a[:, k * LANES:(k + 1) * LANES])
            b_lb.append(b[:, k * LANES:(k + 1) * LANES])

    h_last, c_next = [], []
    for lb in range(nlb):
        cols = slice(lb * LANES, (lb + 1) * LANES)
        step = lambda v, j: v[j * nseg:(j + 1) * nseg, :]
        if chained:
            hs, ap = jnp.zeros((nseg, LANES), F32), jnp.ones((nseg, LANES), F32)
            hs_l, ap_l = [], []
            for j in range(SEG):
                a = step(a_lb[lb], j)
                hs = a * hs + step(b_lb[lb], j)
                ap = a * ap
                hs_l.append(hs)
                ap_l.append(ap)
            c = carry_h[:, cols]
            cin = []
            for r in range(nseg):
                cin.append(c)
                c = ap[r:r + 1, :] * c + hs[r:r + 1, :]
            c_next.append(c)
            cin = jnp.concatenate(cin, axis=0)
            h_l = [hs_l[j] + ap_l[j] * cin for j in range(SEG)]
        else:
            h, h_l = h0_ref[:, cols], []
            for j in range(SEG):
                h = step(a_lb[lb], j) * h + step(b_lb[lb], j)
                h_l.append(h)
            h_last.append(h)
        for j in range(SEG):
            hsl[lb, seg_rows(j), :] = h_l[j]

    h_nat = jnp.concatenate(
        [jnp.concatenate([hsl[lb, r * SEG_PITCH + SUBLANES:(r + 1) * SEG_PITCH, :] for r in range(nseg)], axis=0)
         for lb in range(nlb)], axis=1)
    br_nat = _dot((h_nat * gy).astype(BF16), wbr_ref[...])

    if chained:
        return br_nat, None, jnp.concatenate(c_next, axis=1)
    conv_tail = [jnp.concatenate([u_tail[lb][k] for lb in range(nlb)], axis=1) for k in range(taps)]
    return br_nat, conv_tail, jnp.concatenate(h_last, axis=1)


def _proj_lru_kernel(x_ref, gmix_ref, wuy_ref, wq_ref, wg_ref, wkv_ref, bg_ref, gq_ref, gk_ref, gn_ref,
                     cs_ref, h0_ref, wc_ref, bc_ref, wai_ref, ba_ref, bi_ref, lam_ref, wbr_ref,
                     z_ref, q_ref, g1_ref, k_ref, v_ref, cso_ref, ho_ref,
                     us, hsl, carry_u, carry_h, *, nseg, chained):
    t = pl.program_id(1)
    tm = nseg * SEG
    taps = LRU_CONV_W - 1
    if chained:
        @pl.when(t == 0)
        def _():
            carry_u[...] = cs_ref[...]
            carry_h[...] = h0_ref[...]

    x = x_ref[...]
    xn = (x * _rms_scale(x) * gmix_ref[...]).astype(BF16)

    u = _dot(xn, wuy_ref[:, :D_LRU])
    gy = jax.nn.gelu(_dot(xn, wuy_ref[:, D_LRU:]))
    br, conv_tail, h_end = _lru_body(
        u, gy, cs_ref, h0_ref, wc_ref, bc_ref, wai_ref, ba_ref, bi_ref, lam_ref, wbr_ref,
        us, hsl, carry_u, carry_h, nseg=nseg, chained=chained)
    gates = jax.nn.sigmoid(_dot(xn, wg_ref[...]) + bg_ref[...])
    z_ref[...] = gates[:, :D_MODEL] * br
    g1_ref[...] = gates[:, D_MODEL:].astype(BF16)

    q = _dot(xn, wq_ref[...])
    kv = _dot(xn, wkv_ref[...])
    scale = HEAD_DIM ** -0.5
    for j in range(D_MODEL // MXU_TILE):
        cols = slice(j * MXU_TILE, (j + 1) * MXU_TILE)
        q_ref[:, cols] = (_head_norm(q[:, cols], gq_ref[...], gn_ref[...]) * scale).astype(BF16)
    k_ref[...] = _head_norm(kv[:, :D_KV], gk_ref[...], gn_ref[...])
    v_ref[...] = kv[:, D_KV:]

    if chained:
        carry_u[...] = u[tm - SUBLANES:tm, :]
        carry_h[...] = h_end

        @pl.when(t == pl.num_programs(1) - 1)
        def _():
            cso_ref[...] = u[tm - taps:tm, :]
            ho_ref[...] = h_end
    else:
        for k in range(taps):
            cso_ref[k] = conv_tail[k]
        ho_ref[...] = h_end


def _proj_lru(x, conv_state, h_state, lw, b, t_len, chained):
    taps = LRU_CONV_W - 1
    if chained:
        nseg = SUBLANES
        tm = nseg * SEG
        nt = t_len // tm
        grid = (b, nt)
        row = lambda w: pl.BlockSpec((tm, w), lambda bi, ti: (bi * nt + ti, 0))
        per_b = lambda r: pl.BlockSpec((None, r, D_LRU), lambda bi, ti: (bi, 0, 0))
        state_specs = [per_b(SUBLANES), per_b(1)]
        out_state_specs = [per_b(taps), per_b(1)]
        out_state_shapes = [jax.ShapeDtypeStruct((b, taps, D_LRU), F32), jax.ShapeDtypeStruct((b, 1, D_LRU), F32)]
    else:
        assert t_len == SEG
        nseg = SUBLANES if b % SUBLANES == 0 else b
        tm = nseg * SEG
        grid = (b // nseg, 1)
        row = lambda w: pl.BlockSpec((tm, w), lambda bi, ti: (bi, 0))
        state_specs = [pl.BlockSpec((nseg, SUBLANES, D_LRU), lambda bi, ti: (bi, 0, 0)),
                       pl.BlockSpec((nseg, D_LRU), lambda bi, ti: (bi, 0))]
        out_state_specs = [pl.BlockSpec((taps, nseg, D_LRU), lambda bi, ti: (0, bi, 0)),
                           pl.BlockSpec((nseg, D_LRU), lambda bi, ti: (bi, 0))]
        out_state_shapes = [jax.ShapeDtypeStruct((taps, b, D_LRU), F32), jax.ShapeDtypeStruct((b, D_LRU), F32)]
    m = b * t_len
    vec = _const_spec((1, D_LRU))
    slab = pltpu.VMEM((D_LRU // LANES, nseg * SEG_PITCH, LANES), F32)
    kern = functools.partial(_proj_lru_kernel, nseg=nseg, chained=chained)
    return pl.pallas_call(
        kern,
        grid=grid,
        in_specs=[row(D_MODEL), _const_spec((1, D_MODEL)), _const_spec((D_MODEL, 2 * D_LRU)),
                  _const_spec((D_MODEL, D_MODEL)), _const_spec((D_MODEL, 2 * D_MODEL)),
                  _const_spec((D_MODEL, 2 * D_KV)), _const_spec((1, 2 * D_MODEL)), _const_spec((1, MXU_TILE)), _const_spec((1, MXU_TILE)),
                  _const_spec((MXU_TILE, MXU_TILE))] + state_specs
                 + [_const_spec((LRU_CONV_W, SUBLANES, D_LRU)), _const_spec((SUBLANES, D_LRU)),
                    _const_spec((N_GATE_GROUPS, MXU_TILE, 2 * MXU_TILE)), vec, vec, vec,
                    _const_spec((D_LRU, D_MODEL))],
        out_specs=[row(D_MODEL), row(D_MODEL), row(D_MODEL), row(D_KV), row(D_KV)] + out_state_specs,
        out_shape=[jax.ShapeDtypeStruct((m, D_MODEL), F32), jax.ShapeDtypeStruct((m, D_MODEL), BF16),
                   jax.ShapeDtypeStruct((m, D_MODEL), BF16), jax.ShapeDtypeStruct((m, D_KV), F32),
                   jax.ShapeDtypeStruct((m, D_KV), F32)] + out_state_shapes,
        scratch_shapes=[slab, slab, pltpu.VMEM((SUBLANES, D_LRU), F32), pltpu.VMEM((1, D_LRU), F32)],
        compiler_params=pltpu.CompilerParams(vmem_limit_bytes=VMEM_LIMIT),
        name="proj_lru",
    )(x, lw["g_mix"], lw["w_uy"], lw["w_q"], lw["w_gates"], lw["w_kv"], lw["b_gate"], lw["g_q"], lw["g_k"], lw["gn"], conv_state, h_state,
      lw["w_lconv"], lw["b_lconv"], lw["w_ai"], lw["b_a"], lw["b_i"], lw["lam"], lw["w_br_lru"])


def _attn_ffn_kernel(q_ref, kh_ref, vh_ref, kt_ref, vt_ref, fill_ref, z_ref, g1_ref, x_ref,
                     wba_ref, wout_ref, gffn_ref, fs_ref, wup_ref, wfc_ref, bfc_ref, wdn_ref,
                     y_ref, fso_ref, kbuf, vbuf, attn_s, gbuf, *, tq, chunk, chained):
    ti = pl.program_id(1)
    nunits = tq // chunk
    win = WINDOW + chunk
    pad = MXU_TILE - win
    col_head = lax.broadcasted_iota(jnp.int32, (1, D_KV), 1) // HEAD_DIM
    for src_h, src_t, buf in ((kh_ref, kt_ref, kbuf), (vh_ref, vt_ref, vbuf)):
        hist, rows = src_h[...].astype(BF16), src_t[...].astype(BF16)
        for kh in range(N_KV_HEADS):
            m = (col_head == kh).astype(BF16)
            if chained:
                buf[kh, 0:WINDOW, :] = hist * m
                buf[kh, WINDOW:, :] = rows * m
            else:
                for s in range(nunits):
                    buf[kh, s * win:s * win + WINDOW, :] = hist[s * WINDOW:(s + 1) * WINDOW, :] * m
                    buf[kh, s * win + WINDOW:(s + 1) * win, :] = rows[s * chunk:(s + 1) * chunk, :] * m

    zero_keys = jnp.zeros((pad, D_KV), BF16)
    key_pos = lax.broadcasted_iota(jnp.int32, (1, MXU_TILE), 1)
    for c in range(nunits):
        r0 = c * chunk
        k0 = r0 if chained else c * win
        qst = jnp.concatenate(
            [q_ref[r0:r0 + chunk, g * D_KV:(g + 1) * D_KV] for g in range(GROUP)], axis=0)

        def keys(buf):
            return jnp.concatenate(
                [blk for kh in range(N_KV_HEADS) for blk in (zero_keys, buf[kh, k0:k0 + win, :])], axis=0)

        s_all = lax.dot_general(qst, keys(kbuf), (((1,), (1,)), ((), ())), preferred_element_type=F32)
        valid = key_pos >= pad
        if chained and r0 < WINDOW:
            valid = jnp.logical_and(valid, jnp.logical_or(key_pos >= pad + WINDOW - r0, ti > 0))
        probs = []
        for kh in range(N_KV_HEADS):
            s = jnp.where(valid, s_all[:, kh * MXU_TILE:(kh + 1) * MXU_TILE], fill_ref[kh])
            e = jnp.exp(s - jnp.max(s, axis=-1, keepdims=True))
            probs.append((e * (1.0 / jnp.sum(e, axis=-1, keepdims=True))).astype(BF16))
        o = _dot(jnp.concatenate(probs, axis=1), keys(vbuf))
        for g in range(GROUP):
            attn_s[r0:r0 + chunk, g * D_KV:(g + 1) * D_KV] = o[g * chunk:(g + 1) * chunk, :].astype(BF16)

    br = _dot(attn_s[...], wba_ref[...])
    mix = (z_ref[...] + g1_ref[...].astype(F32) * br).astype(BF16)
    h = x_ref[...] + _dot(mix, wout_ref[...])
    hn = (h * _rms_scale(h) * gffn_ref[...]).astype(BF16)

    taps = FFN_CONV_W - 1
    nstreams = 1 if chained else nunits
    seg = tq // nstreams
    pitch = seg + SUBLANES
    if chained:
        @pl.when(ti == 0)
        def _():
            gbuf[0:SUBLANES, :] = fs_ref[...]
    gate = _dot(hn, wup_ref[:, :D_FF])
    up = _dot(hn, wup_ref[:, D_FF:])
    parts = []
    for r in range(nstreams):
        base = r * pitch
        if not chained:
            gbuf[base:base + SUBLANES, :] = fs_ref[r]
        gbuf[base + SUBLANES:base + pitch, :] = gate[r * seg:(r + 1) * seg, :]
        gc = bfc_ref[...]
        for j in range(FFN_CONV_W):
            lo = base + SUBLANES - taps + j
            gc = gc + wfc_ref[j:j + 1, :] * gbuf[lo:lo + seg, :]
        parts.append(gc)
    gc = parts[0] if nstreams == 1 else jnp.concatenate(parts, axis=0)
    act = (jax.nn.gelu(gc) * up).astype(BF16)
    y_ref[...] = h + _dot(act, wdn_ref[...])

    if chained:
        gbuf[0:SUBLANES, :] = gbuf[tq:tq + SUBLANES, :]

        @pl.when(ti == pl.num_programs(1) - 1)
        def _():
            fso_ref[...] = gbuf[SUBLANES - taps:SUBLANES, :]
    else:
        for r in range(nstreams):
            fso_ref[r] = gbuf[(r + 1) * pitch - taps:(r + 1) * pitch, :]


def _attn_ffn(q, k, v, k_hist, v_hist, z, g1, x, ffn_state, lw, b, t_len, chunk, chained):
    taps = FFN_CONV_W - 1
    if chained:
        tq, ns = _pick(t_len, ROW_TILE), 1
        nt = t_len // tq
        grid = (b, nt)
        row = lambda w: pl.BlockSpec((tq, w), lambda bi, ti: (bi * nt + ti, 0))
        per_b, per_t = t_len // WINDOW, tq // WINDOW
        hist = pl.BlockSpec((WINDOW, D_KV), lambda bi, ti: (jnp.maximum(bi * per_b + ti * per_t - 1, 0), 0))
        kv_rows = WINDOW + tq
        state = lambda r: pl.BlockSpec((None, r, D_FF), lambda bi, ti: (bi, 0, 0))
    else:
        assert t_len == chunk
        ns = SUBLANES if b % SUBLANES == 0 else b
        tq = ns * chunk
        grid = (b // ns, 1)
        row = lambda w: pl.BlockSpec((tq, w), lambda bi, ti: (bi, 0))
        hist = pl.BlockSpec((ns * WINDOW, D_KV), lambda bi, ti: (bi, 0))
        kv_rows = ns * (WINDOW + chunk)
        state = lambda r: pl.BlockSpec((ns, r, D_FF), lambda bi, ti: (bi, 0, 0))
    kern = functools.partial(_attn_ffn_kernel, tq=tq, chunk=chunk, chained=chained)
    return pl.pallas_call(
        kern,
        grid=grid,
        in_specs=[row(D_MODEL), hist, hist, row(D_KV), row(D_KV),
                  _const_spec((N_KV_HEADS, GROUP * chunk, MXU_TILE)),
                  row(D_MODEL), row(D_MODEL), row(D_MODEL),
                  _const_spec((D_MODEL, D_MODEL)), _const_spec((D_MODEL, D_MODEL)), _const_spec((1, D_MODEL)),
                  state(SUBLANES), _const_spec((D_MODEL, 2 * D_FF)),
                  _const_spec((FFN_CONV_W, D_FF)), _const_spec((1, D_FF)), _const_spec((D_FF, D_MODEL))],
        out_specs=[row(D_MODEL), state(taps)],
        out_shape=[jax.ShapeDtypeStruct((b * t_len, D_MODEL), F32),
                   jax.ShapeDtypeStruct((b, taps, D_FF), F32)],
        scratch_shapes=[pltpu.VMEM((N_KV_HEADS, kv_rows, D_KV), BF16)] * 2
                       + [pltpu.VMEM((tq, D_MODEL), BF16),
                          pltpu.VMEM((ns * (tq // ns + SUBLANES), D_FF), F32)],
        compiler_params=pltpu.CompilerParams(vmem_limit_bytes=VMEM_LIMIT),
        name="attn_ffn",
    )(q, k_hist, v_hist, k, v, lw["sink_fill"][chunk], z, g1, x, lw["w_br_attn"], lw["w_out"], lw["g_ffn"],
      ffn_state, lw["w_up"], lw["w_fconv"], lw["b_fconv"], lw["w_down"])


def _block_diag_groups(w):
    per = MXU_TILE // LRU_BLOCK
    w4 = w.reshape(N_GATE_GROUPS, per, LRU_BLOCK, LRU_BLOCK)
    return jnp.einsum("jncd,nm->jncmd", w4, jnp.eye(per, dtype=w.dtype)).reshape(
        N_GATE_GROUPS, MXU_TILE, MXU_TILE)


def _prep_layer(l, p, chunks):
    w_in = p["w_in"][l]
    wq = w_in[:, _Q0:_K0].reshape(D_MODEL, N_KV_HEADS, GROUP, HEAD_DIM).transpose(0, 2, 1, 3)
    w_ba = p["w_br_attn"][l].reshape(N_KV_HEADS, GROUP, HEAD_DIM, D_MODEL).transpose(1, 0, 2, 3)
    per = MXU_TILE // HEAD_DIM
    sink = p["attn_sink"][l].reshape(N_KV_HEADS, GROUP)
    row = lambda v: v.reshape(1, -1)
    return {
        "g_mix": row(p["g_mix"][l]),
        "w_uy": w_in[:, :_Q0].astype(BF16),
        "w_q": wq.reshape(D_MODEL, D_MODEL).astype(BF16),
        "w_gates": w_in[:, _G0:].astype(BF16),
        "w_kv": w_in[:, _K0:_G0].astype(BF16),
        "b_gate": row(p["b_gate"][l]),
        "g_q": row(jnp.tile(p["g_q"][l], per)),
        "g_k": row(jnp.tile(p["g_k"][l], per)),
        "gn": (jnp.kron(jnp.eye(per, dtype=F32), jnp.ones((HEAD_DIM, HEAD_DIM), F32)) / HEAD_DIM).astype(BF16),
        "w_lconv": jnp.broadcast_to(p["w_lru_conv"][l][:, None, :], (LRU_CONV_W, SUBLANES, D_LRU)),
        "b_lconv": jnp.broadcast_to(p["b_lru_conv"][l][None, :], (SUBLANES, D_LRU)),
        "w_ai": jnp.concatenate([_block_diag_groups(p["w_lru_a"][l]), _block_diag_groups(p["w_lru_i"][l])],
                                axis=2).astype(BF16),
        "b_a": row(p["b_lru_a"][l]),
        "b_i": row(p["b_lru_i"][l]),
        "lam": row(p["lru_lambda"][l]),
        "w_br_lru": p["w_br_lru"][l].astype(BF16),
        "sink_fill": {c: jnp.full((N_KV_HEADS, GROUP * c, MXU_TILE), NEG, F32).at[:, :, 0].set(
            jnp.repeat(sink, c, axis=1)) for c in chunks},
        "w_br_attn": w_ba.reshape(D_MODEL, D_MODEL).astype(BF16),
        "w_out": p["w_out"][l].astype(BF16),
        "g_ffn": row(p["g_ffn"][l]),
        "w_up": p["w_up"][l].astype(BF16),
        "w_fconv": p["w_ffn_conv"][l],
        "b_fconv": row(p["b_ffn_conv"][l]),
        "w_down": p["w_down"][l].astype(BF16),
    }


def _pick(n, pref):
    return pref if n % pref == 0 else n


def _layer(x, lw, conv_state, h_state, ffn_state, k_hist, v_hist, b, t_len, chunk, chained):
    z, q, g1, k, v, conv_new, h_new = _proj_lru(x, conv_state, h_state, lw, b, t_len, chained)
    if chained:
        k_hist, v_hist = k, v
    y, ffn_new = _attn_ffn(q, k, v, k_hist, v_hist, z, g1, x, ffn_state, lw, b, t_len, chunk, chained)
    return y, k, v, conv_new, h_new, ffn_new


def _last_window(kv, b, t_len):
    return kv.reshape(b, t_len, D_KV)[:, t_len - WINDOW:].reshape(b, WINDOW, N_KV_HEADS, HEAD_DIM)


def _front_pad(state):
    return jnp.pad(state, ((0, 0), (SUBLANES - state.shape[1], 0), (0, 0)))


def kernel(x_prompt, x_sample, state_lru_conv, state_lru_h, cache_k, cache_v, state_ffn_conv, g_mix, w_in, b_gate, w_lru_conv, b_lru_conv, w_lru_a, b_lru_a, w_lru_i, b_lru_i, lru_lambda, g_q, g_k, attn_sink, w_br_lru, w_br_attn, w_out, g_ffn, w_up, w_ffn_conv, b_ffn_conv, w_down):
    params = dict(g_mix=g_mix, w_in=w_in, b_gate=b_gate, w_lru_conv=w_lru_conv, b_lru_conv=b_lru_conv,
                  w_lru_a=w_lru_a, b_lru_a=b_lru_a, w_lru_i=w_lru_i, b_lru_i=b_lru_i, lru_lambda=lru_lambda,
                  g_q=g_q, g_k=g_k, attn_sink=attn_sink, w_br_lru=w_br_lru, w_br_attn=w_br_attn, w_out=w_out,
                  g_ffn=g_ffn, w_up=w_up, w_ffn_conv=w_ffn_conv, b_ffn_conv=b_ffn_conv, w_down=w_down)
    depth = w_in.shape[0]
    bp, sp, _ = x_prompt.shape
    bs, ss, _ = x_sample.shape
    p_chunk, s_chunk = PROMPT_CHUNK, ss
    xp = x_prompt.reshape(bp * sp, D_MODEL)
    xs = x_sample.reshape(bs * ss, D_MODEL)
    zeros_p = (jnp.zeros((bp, SUBLANES, D_LRU), F32), jnp.zeros((bp, 1, D_LRU), F32),
               jnp.zeros((bp, SUBLANES, D_FF), F32))
    outs = {n: [] for n in ("p_lc", "p_lh", "p_k", "p_v", "p_fc", "s_lc", "s_lh", "s_k", "s_v", "s_fc")}
    for l in range(depth):
        lw = _prep_layer(l, params, {p_chunk, s_chunk})
        xp, k, v, lc, lh, fc = _layer(xp, lw, *zeros_p, None, None, bp, sp, p_chunk, True)
        outs["p_lc"].append(lc)
        outs["p_lh"].append(lh.reshape(bp, D_LRU))
        outs["p_k"].append(_last_window(k, bp, sp))
        outs["p_v"].append(_last_window(v, bp, sp))
        outs["p_fc"].append(fc)
        xs, k, v, lc, lh, fc = _layer(
            xs, lw, _front_pad(state_lru_conv[l]), state_lru_h[l], _front_pad(state_ffn_conv[l]),
            cache_k[l].reshape(bs * WINDOW, D_KV), cache_v[l].reshape(bs * WINDOW, D_KV),
            bs, ss, s_chunk, False)
        outs["s_lc"].append(lc.transpose(1, 0, 2))
        outs["s_lh"].append(lh.reshape(bs, D_LRU))
        outs["s_k"].append(k.reshape(bs, ss, N_KV_HEADS, HEAD_DIM))
        outs["s_v"].append(v.reshape(bs, ss, N_KV_HEADS, HEAD_DIM))
        outs["s_fc"].append(fc)
    st = {n: jnp.stack(v) for n, v in outs.items()}
    return (xp.reshape(bp, sp, D_MODEL), xs.reshape(bs, ss, D_MODEL),
            st["p_lc"], st["p_lh"], st["p_k"], st["p_v"], st["p_fc"],
            st["s_lc"], st["s_lh"], st["s_k"], st["s_v"], st["s_fc"])
```

```python
import functools

import jax
import jax.numpy as jnp
from jax import lax
from jax.experimental import pallas as pl
from jax.experimental.pallas import tpu as pltpu

D_MODEL = 1024
N_HEADS = 16
N_KV_HEADS = 4
HEAD_DIM = 64
GROUP = N_HEADS // N_KV_HEADS
D_KV = N_KV_HEADS * HEAD_DIM
WINDOW = 128
D_LRU = D_MODEL
N_LRU_BLOCKS = 16
LRU_BLOCK = D_LRU // N_LRU_BLOCKS
LRU_C = 8.0
LRU_CONV_W = 4
D_FF = 3 * D_MODEL
FFN_CONV_W = 3
EPS = 1e-6
NEG = -1e30

MXU_TILE = 256
SUBLANES = 8
LANES = 128
SEG = 32
ROW_TILE = 256
PROMPT_CHUNK = 64
SEG_PITCH = SEG + SUBLANES
TINY = 1e-30
N_GATE_GROUPS = D_LRU // MXU_TILE

_Q0, _K0, _G0 = 2 * D_LRU, 2 * D_LRU + D_MODEL, 2 * D_LRU + D_MODEL + 2 * D_KV

VMEM_LIMIT = 56 * 1024 * 1024

BF16 = jnp.bfloat16
F32 = jnp.float32


def _dot(a, b):
    return jnp.dot(a, b, preferred_element_type=F32)


def _rms_scale(x):
    return lax.rsqrt(jnp.mean(x * x, axis=-1, keepdims=True) + EPS)


def _const_spec(shape):
    nd = len(shape)
    return pl.BlockSpec(shape, lambda *_: (0,) * nd, pipeline_mode=pl.Buffered(1))


def _head_norm(t, gain, gn):
    ms = _dot((t * t).astype(BF16), gn)
    return t * lax.rsqrt(ms + EPS) * gain


def _lru_body(u, gy_fn, cs_ref, h0_ref, wc_ref, bc_ref, wai_ref, ba_ref, bi_ref, lam_ref, wbr_ref,
              us, hsl, carry_u, carry_h, *, nseg, chained):
    nlb = D_LRU // LANES
    taps = LRU_CONV_W - 1

    for lb in range(nlb):
        cols = slice(lb * LANES, (lb + 1) * LANES)
        for r in range(nseg):
            base = r * SEG_PITCH
            rows = slice(r * SEG, (r + 1) * SEG)
            if not chained:
                prev = cs_ref[r, :, cols]
            elif r == 0:
                prev = carry_u[:, cols]
            else:
                prev = u[r * SEG - SUBLANES:r * SEG, cols]
            us[lb, base:base + SUBLANES, :] = prev
            us[lb, base + SUBLANES:base + SEG_PITCH, :] = u[rows, cols]

    def seg_rows(i):
        return pl.ds(SUBLANES + i, nseg, stride=SEG_PITCH)

    xc_lb, u_tail = [], []
    for lb in range(nlb):
        cols = slice(lb * LANES, (lb + 1) * LANES)
        w = [wc_ref[k, 0:nseg, cols] for k in range(LRU_CONV_W)]
        bias = bc_ref[0:nseg, cols]
        ut = [us[lb, seg_rows(i - taps), :] for i in range(SEG + taps)]
        steps = []
        for j in range(SEG):
            x = bias
            for k in range(LRU_CONV_W):
                x = x + w[k] * ut[j + k]
            steps.append(x)
        xc_lb.append(jnp.concatenate(steps, axis=0))
        u_tail.append(ut[SEG:])
    xc = jnp.concatenate(xc_lb, axis=1)

    lam = lam_ref[...]
    softplus_neg_lam = jnp.maximum(-lam, 0.0) + jnp.log1p(jnp.exp(-jnp.abs(lam)))
    log_a_per_r = -LRU_C * softplus_neg_lam
    xcb = xc.astype(BF16)
    a_lb, b_lb = [], []
    for j in range(N_GATE_GROUPS):
        cols = slice(j * MXU_TILE, (j + 1) * MXU_TILE)
        ri = _dot(xcb[:, cols], wai_ref[j])
        r = jax.nn.sigmoid(ri[:, :MXU_TILE] + ba_ref[:, cols])
        i = jax.nn.sigmoid(ri[:, MXU_TILE:] + bi_ref[:, cols])
        log_a = r * log_a_per_r[:, cols]
        a = jnp.exp(log_a)
        one_minus_a2 = -jnp.tanh(log_a) * (a * a + 1.0)
        mult = one_minus_a2 * lax.rsqrt(jnp.maximum(one_minus_a2, TINY))
        b = mult * i * xc[:, cols]
        for k in range(MXU_TILE // LANES):
            a_lb.append(a[:, k * LANES:(k + 1) * LANES])
            b_lb.append(b[:, k * LANES:(k + 1) * LANES])

    h_last, c_next = [], []
    for lb in range(nlb):
        cols = slice(lb * LANES, (lb + 1) * LANES)
        step = lambda v, j: v[j * nseg:(j + 1) * nseg, :]
        if chained:
            hs, ap = jnp.zeros((nseg, LANES), F32), jnp.ones((nseg, LANES), F32)
            hs_l, ap_l = [], []
            for j in range(SEG):
                a = step(a_lb[lb], j)
                hs = a * hs + step(b_lb[lb], j)
                ap = a * ap
                hs_l.append(hs)
                ap_l.append(ap)
            c = carry_h[:, cols]
            cin = []
            for r in range(nseg):
                cin.append(c)
                c = ap[r:r + 1, :] * c + hs[r:r + 1, :]
            c_next.append(c)
            cin = jnp.concatenate(cin, axis=0)
            h_l = [hs_l[j] + ap_l[j] * cin for j in range(SEG)]
        else:
            h, h_l = h0_ref[:, cols], []
            for j in range(SEG):
                h = step(a_lb[lb], j) * h + step(b_lb[lb], j)
                h_l.append(h)
            h_last.append(h)
        for j in range(SEG):
            hsl[lb, seg_rows(j), :] = h_l[j]

    h_nat = jnp.concatenate(
        [jnp.concatenate([hsl[lb, r * SEG_PITCH + SUBLANES:(r + 1) * SEG_PITCH, :] for r in range(nseg)], axis=0)
         for lb in range(nlb)], axis=1)
    br_nat = _dot((h_nat * gy_fn()).astype(BF16), wbr_ref[...])

    if chained:
        return br_nat, None, jnp.concatenate(c_next, axis=1)
    conv_tail = [jnp.concatenate([u_tail[lb][k] for lb in range(nlb)], axis=1) for k in range(taps)]
    return br_nat, conv_tail, jnp.concatenate(h_last, axis=1)


def _proj_lru_kernel(x_ref, gmix_ref, wuy_ref, wq_ref, wg_ref, wkv_ref, bg_ref, gq_ref, gk_ref, gn_ref,
                     cs_ref, h0_ref, wc_ref, bc_ref, wai_ref, ba_ref, bi_ref, lam_ref, wbr_ref,
                     z_ref, q_ref, g1_ref, k_ref, v_ref, cso_ref, ho_ref,
                     us, hsl, carry_u, carry_h, *, nseg, chained):
    t = pl.program_id(1)
    tm = nseg * SEG
    taps = LRU_CONV_W - 1
    if chained:
        @pl.when(t == 0)
        def _():
            carry_u[...] = cs_ref[...]
            carry_h[...] = h0_ref[...]

    x = x_ref[...]
    xn = (x * _rms_scale(x) * gmix_ref[...]).astype(BF16)

    u = _dot(xn, wuy_ref[:, :D_LRU])
    br, conv_tail, h_end = _lru_body(
        u, lambda: jax.nn.gelu(_dot(xn, wuy_ref[:, D_LRU:])), cs_ref, h0_ref, wc_ref, bc_ref, wai_ref, ba_ref, bi_ref, lam_ref, wbr_ref,
        us, hsl, carry_u, carry_h, nseg=nseg, chained=chained)
    gates = jax.nn.sigmoid(_dot(xn, wg_ref[...]) + bg_ref[...])
    z_ref[...] = gates[:, :D_MODEL] * br
    g1_ref[...] = gates[:, D_MODEL:].astype(BF16)

    q = _dot(xn, wq_ref[...])
    kv = _dot(xn, wkv_ref[...])
    scale = HEAD_DIM ** -0.5
    for j in range(D_MODEL // MXU_TILE):
        cols = slice(j * MXU_TILE, (j + 1) * MXU_TILE)
        q_ref[:, cols] = (_head_norm(q[:, cols], gq_ref[...], gn_ref[...]) * scale).astype(BF16)
    k_ref[...] = _head_norm(kv[:, :D_KV], gk_ref[...], gn_ref[...])
    v_ref[...] = kv[:, D_KV:]

    if chained:
        carry_u[...] = u[tm - SUBLANES:tm, :]
        carry_h[...] = h_end

        @pl.when(t == pl.num_programs(1) - 1)
        def _():
            cso_ref[...] = u[tm - taps:tm, :]
            ho_ref[...] = h_end
    else:
        for k in range(taps):
            cso_ref[k] = conv_tail[k]
        ho_ref[...] = h_end


def _proj_lru(x, conv_state, h_state, lw, b, t_len, chained):
    taps = LRU_CONV_W - 1
    if chained:
        nseg = SUBLANES
        tm = nseg * SEG
        nt = t_len // tm
        grid = (b, nt)
        row = lambda w: pl.BlockSpec((tm, w), lambda bi, ti: (bi * nt + ti, 0))
        per_b = lambda r: pl.BlockSpec((None, r, D_LRU), lambda bi, ti: (bi, 0, 0))
        state_specs = [per_b(SUBLANES), per_b(1)]
        out_state_specs = [per_b(taps), per_b(1)]
        out_state_shapes = [jax.ShapeDtypeStruct((b, taps, D_LRU), F32), jax.ShapeDtypeStruct((b, 1, D_LRU), F32)]
    else:
        assert t_len == SEG
        nseg = SUBLANES if b % SUBLANES == 0 else b
        tm = nseg * SEG
        grid = (b // nseg, 1)
        row = lambda w: pl.BlockSpec((tm, w), lambda bi, ti: (bi, 0))
        state_specs = [pl.BlockSpec((nseg, SUBLANES, D_LRU), lambda bi, ti: (bi, 0, 0)),
                       pl.BlockSpec((nseg, D_LRU), lambda bi, ti: (bi, 0))]
        out_state_specs = [pl.BlockSpec((taps, nseg, D_LRU), lambda bi, ti: (0, bi, 0)),
                           pl.BlockSpec((nseg, D_LRU), lambda bi, ti: (bi, 0))]
        out_state_shapes = [jax.ShapeDtypeStruct((taps, b, D_LRU), F32), jax.ShapeDtypeStruct((b, D_LRU), F32)]
    m = b * t_len
    vec = _const_spec((1, D_LRU))
    slab = pltpu.VMEM((D_LRU // LANES, nseg * SEG_PITCH, LANES), F32)
    kern = functools.partial(_proj_lru_kernel, nseg=nseg, chained=chained)
    return pl.pallas_call(
        kern,
        grid=grid,
        in_specs=[row(D_MODEL), _const_spec((1, D_MODEL)), _const_spec((D_MODEL, 2 * D_LRU)),
                  _const_spec((D_MODEL, D_MODEL)), _const_spec((D_MODEL, 2 * D_MODEL)),
                  _const_spec((D_MODEL, 2 * D_KV)), _const_spec((1, 2 * D_MODEL)), _const_spec((1, MXU_TILE)), _const_spec((1, MXU_TILE)),
                  _const_spec((MXU_TILE, MXU_TILE))] + state_specs
                 + [_const_spec((LRU_CONV_W, SUBLANES, D_LRU)), _const_spec((SUBLANES, D_LRU)),
                    _const_spec((N_GATE_GROUPS, MXU_TILE, 2 * MXU_TILE)), vec, vec, vec,
                    _const_spec((D_LRU, D_MODEL))],
        out_specs=[row(D_MODEL), row(D_MODEL), row(D_MODEL), row(D_KV), row(D_KV)] + out_state_specs,
        out_shape=[jax.ShapeDtypeStruct((m, D_MODEL), F32), jax.ShapeDtypeStruct((m, D_MODEL), BF16),
                   jax.ShapeDtypeStruct((m, D_MODEL), BF16), jax.ShapeDtypeStruct((m, D_KV), F32),
                   jax.ShapeDtypeStruct((m, D_KV), F32)] + out_state_shapes,
        scratch_shapes=[slab, slab, pltpu.VMEM((SUBLANES, D_LRU), F32), pltpu.VMEM((1, D_LRU), F32)],
        compiler_params=pltpu.CompilerParams(vmem_limit_bytes=VMEM_LIMIT),
        name="proj_lru",
    )(x, lw["g_mix"], lw["w_uy"], lw["w_q"], lw["w_gates"], lw["w_kv"], lw["b_gate"], lw["g_q"], lw["g_k"], lw["gn"], conv_state, h_state,
      lw["w_lconv"], lw["b_lconv"], lw["w_ai"], lw["b_a"], lw["b_i"], lw["lam"], lw["w_br_lru"])


def _attn_ffn_kernel(q_ref, kh_ref, vh_ref, kt_ref, vt_ref, fill_ref, z_ref, g1_ref, x_ref,
                     wba_ref, wout_ref, gffn_ref, fs_ref, wup_ref, wfc_ref, bfc_ref, wdn_ref,
                     y_ref, fso_ref, kbuf, vbuf, attn_s, carry_g, *, tq, chunk, chained):
    ti = pl.program_id(1)
    nunits = tq // chunk
    win = WINDOW + chunk
    pad = MXU_TILE - win
    col_head = lax.broadcasted_iota(jnp.int32, (1, D_KV), 1) // HEAD_DIM
    for src_h, src_t, buf in ((kh_ref, kt_ref, kbuf), (vh_ref, vt_ref, vbuf)):
        hist, rows = src_h[...].astype(BF16), src_t[...].astype(BF16)
        for kh in range(N_KV_HEADS):
            m = (col_head == kh).astype(BF16)
            if chained:
                buf[kh, 0:WINDOW, :] = hist * m
                buf[kh, WINDOW:, :] = rows * m
            else:
                for s in range(nunits):
                    buf[kh, s * win:s * win + WINDOW, :] = hist[s * WINDOW:(s + 1) * WINDOW, :] * m
                    buf[kh, s * win + WINDOW:(s + 1) * win, :] = rows[s * chunk:(s + 1) * chunk, :] * m

    zero_keys = jnp.zeros((pad, D_KV), BF16)
    key_pos = lax.broadcasted_iota(jnp.int32, (1, MXU_TILE), 1)
    for c in range(nunits):
        r0 = c * chunk
        k0 = r0 if chained else c * win
        qst = jnp.concatenate(
            [q_ref[r0:r0 + chunk, g * D_KV:(g + 1) * D_KV] for g in range(GROUP)], axis=0)

        def keys(buf):
            return jnp.concatenate(
                [blk for kh in range(N_KV_HEADS) for blk in (zero_keys, buf[kh, k0:k0 + win, :])], axis=0)

        s_all = lax.dot_general(qst, keys(kbuf), (((1,), (1,)), ((), ())), preferred_element_type=F32)
        valid = key_pos >= pad
        if chained and r0 < WINDOW:
            valid = jnp.logical_and(valid, jnp.logical_or(key_pos >= pad + WINDOW - r0, ti > 0))
        probs = []
        for kh in range(N_KV_HEADS):
            s = jnp.where(valid, s_all[:, kh * MXU_TILE:(kh + 1) * MXU_TILE], fill_ref[kh])
            e = jnp.exp(s - jnp.max(s, axis=-1, keepdims=True))
            probs.append((e * (1.0 / jnp.sum(e, axis=-1, keepdims=True))).astype(BF16))
        o = _dot(jnp.concatenate(probs, axis=1), keys(vbuf))
        for g in range(GROUP):
            attn_s[r0:r0 + chunk, g * D_KV:(g + 1) * D_KV] = o[g * chunk:(g + 1) * chunk, :].astype(BF16)

    br = _dot(attn_s[...], wba_ref[...])
    mix = (z_ref[...] + g1_ref[...].astype(F32) * br).astype(BF16)
    h = x_ref[...] + _dot(mix, wout_ref[...])
    hn = (h * _rms_scale(h) * gffn_ref[...]).astype(BF16)

    taps = FFN_CONV_W - 1
    nstreams = 1 if chained else nunits
    seg = tq // nstreams
    if chained:
        @pl.when(ti == 0)
        def _():
            carry_g[...] = fs_ref[SUBLANES - taps:SUBLANES, :]
    gate = _dot(hn, wup_ref[:, :D_FF])
    up = _dot(hn, wup_ref[:, D_FF:])
    sub = lax.broadcasted_iota(jnp.int32, (SUBLANES, D_FF), 0)
    gc = bfc_ref[...] + wfc_ref[taps:taps + 1, :] * gate
    for j in range(taps):
        shift = taps - j
        rolled = pltpu.roll(gate, shift, axis=0)
        blocks = []
        for r in range(nstreams):
            hist = carry_g[...] if chained else fs_ref[r, SUBLANES - taps:SUBLANES, :]
            head = rolled[r * seg:r * seg + SUBLANES, :]
            for t in range(shift):
                head = jnp.where(sub == t, hist[taps - shift + t:taps - shift + t + 1, :], head)
            blocks += [head, rolled[r * seg + SUBLANES:(r + 1) * seg, :]]
        gc = gc + wfc_ref[j:j + 1, :] * jnp.concatenate(blocks, axis=0)
    act = (jax.nn.gelu(gc) * up).astype(BF16)
    y_ref[...] = h + _dot(act, wdn_ref[...])

    if chained:
        carry_g[...] = gate[tq - taps:tq, :]

        @pl.when(ti == pl.num_programs(1) - 1)
        def _():
            fso_ref[...] = gate[tq - taps:tq, :]
    else:
        for r in range(nstreams):
            fso_ref[r] = gate[(r + 1) * seg - taps:(r + 1) * seg, :]


def _attn_ffn(q, k, v, k_hist, v_hist, z, g1, x, ffn_state, lw, b, t_len, chunk, chained):
    taps = FFN_CONV_W - 1
    if chained:
        tq, ns = _pick(t_len, ROW_TILE), 1
        nt = t_len // tq
        grid = (b, nt)
        row = lambda w: pl.BlockSpec((tq, w), lambda bi, ti: (bi * nt + ti, 0))
        per_b, per_t = t_len // WINDOW, tq // WINDOW
        hist = pl.BlockSpec((WINDOW, D_KV), lambda bi, ti: (jnp.maximum(bi * per_b + ti * per_t - 1, 0), 0))
        kv_rows = WINDOW + tq
        state = lambda r: pl.BlockSpec((None, r, D_FF), lambda bi, ti: (bi, 0, 0))
    else:
        assert t_len == chunk
        ns = SUBLANES if b % SUBLANES == 0 else b
        tq = ns * chunk
        grid = (b // ns, 1)
        row = lambda w: pl.BlockSpec((tq, w), lambda bi, ti: (bi, 0))
        hist = pl.BlockSpec((ns * WINDOW, D_KV), lambda bi, ti: (bi, 0))
        kv_rows = ns * (WINDOW + chunk)
        state = lambda r: pl.BlockSpec((ns, r, D_FF), lambda bi, ti: (bi, 0, 0))
    kern = functools.partial(_attn_ffn_kernel, tq=tq, chunk=chunk, chained=chained)
    return pl.pallas_call(
        kern,
        grid=grid,
        in_specs=[row(D_MODEL), hist, hist, row(D_KV), row(D_KV),
                  _const_spec((N_KV_HEADS, GROUP * chunk, MXU_TILE)),
                  row(D_MODEL), row(D_MODEL), row(D_MODEL),
                  _const_spec((D_MODEL, D_MODEL)), _const_spec((D_MODEL, D_MODEL)), _const_spec((1, D_MODEL)),
                  state(SUBLANES), _const_spec((D_MODEL, 2 * D_FF)),
                  _const_spec((FFN_CONV_W, D_FF)), _const_spec((1, D_FF)), _const_spec((D_FF, D_MODEL))],
        out_specs=[row(D_MODEL), state(taps)],
        out_shape=[jax.ShapeDtypeStruct((b * t_len, D_MODEL), F32),
                   jax.ShapeDtypeStruct((b, taps, D_FF), F32)],
        scratch_shapes=[pltpu.VMEM((N_KV_HEADS, kv_rows, D_KV), BF16)] * 2
                       + [pltpu.VMEM((tq, D_MODEL), BF16), pltpu.VMEM((FFN_CONV_W - 1, D_FF), F32)],
        compiler_params=pltpu.CompilerParams(vmem_limit_bytes=VMEM_LIMIT),
        name="attn_ffn",
    )(q, k_hist, v_hist, k, v, lw["sink_fill"][chunk], z, g1, x, lw["w_br_attn"], lw["w_out"], lw["g_ffn"],
      ffn_state, lw["w_up"], lw["w_fconv"], lw["b_fconv"], lw["w_down"])


def _block_diag_groups(w):
    per = MXU_TILE // LRU_BLOCK
    w4 = w.reshape(N_GATE_GROUPS, per, LRU_BLOCK, LRU_BLOCK)
    return jnp.einsum("jncd,nm->jncmd", w4, jnp.eye(per, dtype=w.dtype)).reshape(
        N_GATE_GROUPS, MXU_TILE, MXU_TILE)


def _prep_layer(l, p, chunks):
    w_in = p["w_in"][l]
    wq = w_in[:, _Q0:_K0].reshape(D_MODEL, N_KV_HEADS, GROUP, HEAD_DIM).transpose(0, 2, 1, 3)
    w_ba = p["w_br_attn"][l].reshape(N_KV_HEADS, GROUP, HEAD_DIM, D_MODEL).transpose(1, 0, 2, 3)
    per = MXU_TILE // HEAD_DIM
    sink = p["attn_sink"][l].reshape(N_KV_HEADS, GROUP)
    row = lambda v: v.reshape(1, -1)
    return {
        "g_mix": row(p["g_mix"][l]),
        "w_uy": w_in[:, :_Q0].astype(BF16),
        "w_q": wq.reshape(D_MODEL, D_MODEL).astype(BF16),
        "w_gates": w_in[:, _G0:].astype(BF16),
        "w_kv": w_in[:, _K0:_G0].astype(BF16),
        "b_gate": row(p["b_gate"][l]),
        "g_q": row(jnp.tile(p["g_q"][l], per)),
        "g_k": row(jnp.tile(p["g_k"][l], per)),
        "gn": (jnp.kron(jnp.eye(per, dtype=F32), jnp.ones((HEAD_DIM, HEAD_DIM), F32)) / HEAD_DIM).astype(BF16),
        "w_lconv": jnp.broadcast_to(p["w_lru_conv"][l][:, None, :], (LRU_CONV_W, SUBLANES, D_LRU)),
        "b_lconv": jnp.broadcast_to(p["b_lru_conv"][l][None, :], (SUBLANES, D_LRU)),
        "w_ai": jnp.concatenate([_block_diag_groups(p["w_lru_a"][l]), _block_diag_groups(p["w_lru_i"][l])],
                                axis=2).astype(BF16),
        "b_a": row(p["b_lru_a"][l]),
        "b_i": row(p["b_lru_i"][l]),
        "lam": row(p["lru_lambda"][l]),
        "w_br_lru": p["w_br_lru"][l].astype(BF16),
        "sink_fill": {c: jnp.full((N_KV_HEADS, GROUP * c, MXU_TILE), NEG, F32).at[:, :, 0].set(
            jnp.repeat(sink, c, axis=1)) for c in chunks},
        "w_br_attn": w_ba.reshape(D_MODEL, D_MODEL).astype(BF16),
        "w_out": p["w_out"][l].astype(BF16),
        "g_ffn": row(p["g_ffn"][l]),
        "w_up": p["w_up"][l].astype(BF16),
        "w_fconv": p["w_ffn_conv"][l],
        "b_fconv": row(p["b_ffn_conv"][l]),
        "w_down": p["w_down"][l].astype(BF16),
    }


def _pick(n, pref):
    return pref if n % pref == 0 else n


def _layer(x, lw, conv_state, h_state, ffn_state, k_hist, v_hist, b, t_len, chunk, chained):
    z, q, g1, k, v, conv_new, h_new = _proj_lru(x, conv_state, h_state, lw, b, t_len, chained)
    if chained:
        k_hist, v_hist = k, v
    y, ffn_new = _attn_ffn(q, k, v, k_hist, v_hist, z, g1, x, ffn_state, lw, b, t_len, chunk, chained)
    return y, k, v, conv_new, h_new, ffn_new


def _last_window(kv, b, t_len):
    return kv.reshape(b, t_len, D_KV)[:, t_len - WINDOW:].reshape(b, WINDOW, N_KV_HEADS, HEAD_DIM)


def _front_pad(state):
    return jnp.pad(state, ((0, 0), (SUBLANES - state.shape[1], 0), (0, 0)))


def kernel(x_prompt, x_sample, state_lru_conv, state_lru_h, cache_k, cache_v, state_ffn_conv, g_mix, w_in, b_gate, w_lru_conv, b_lru_conv, w_lru_a, b_lru_a, w_lru_i, b_lru_i, lru_lambda, g_q, g_k, attn_sink, w_br_lru, w_br_attn, w_out, g_ffn, w_up, w_ffn_conv, b_ffn_conv, w_down):
    params = dict(g_mix=g_mix, w_in=w_in, b_gate=b_gate, w_lru_conv=w_lru_conv, b_lru_conv=b_lru_conv,
                  w_lru_a=w_lru_a, b_lru_a=b_lru_a, w_lru_i=w_lru_i, b_lru_i=b_lru_i, lru_lambda=lru_lambda,
                  g_q=g_q, g_k=g_k, attn_sink=attn_sink, w_br_lru=w_br_lru, w_br_attn=w_br_attn, w_out=w_out,
                  g_ffn=g_ffn, w_up=w_up, w_ffn_conv=w_ffn_conv, b_ffn_conv=b_ffn_conv, w_down=w_down)
    depth = w_in.shape[0]
    bp, sp, _ = x_prompt.shape
    bs, ss, _ = x_sample.shape
    p_chunk, s_chunk = PROMPT_CHUNK, ss
    xp = x_prompt.reshape(bp * sp, D_MODEL)
    xs = x_sample.reshape(bs * ss, D_MODEL)
    zeros_p = (jnp.zeros((bp, SUBLANES, D_LRU), F32), jnp.zeros((bp, 1, D_LRU), F32),
               jnp.zeros((bp, SUBLANES, D_FF), F32))
    outs = {n: [] for n in ("p_lc", "p_lh", "p_k", "p_v", "p_fc", "s_lc", "s_lh", "s_k", "s_v", "s_fc")}
    for l in range(depth):
        lw = _prep_layer(l, params, {p_chunk, s_chunk})
        xp, k, v, lc, lh, fc = _layer(xp, lw, *zeros_p, None, None, bp, sp, p_chunk, True)
        outs["p_lc"].append(lc)
        outs["p_lh"].append(lh.reshape(bp, D_LRU))
        outs["p_k"].append(_last_window(k, bp, sp))
        outs["p_v"].append(_last_window(v, bp, sp))
        outs["p_fc"].append(fc)
        xs, k, v, lc, lh, fc = _layer(
            xs, lw, _front_pad(state_lru_conv[l]), state_lru_h[l], _front_pad(state_ffn_conv[l]),
            cache_k[l].reshape(bs * WINDOW, D_KV), cache_v[l].reshape(bs * WINDOW, D_KV),
            bs, ss, s_chunk, False)
        outs["s_lc"].append(lc.transpose(1, 0, 2))
        outs["s_lh"].append(lh.reshape(bs, D_LRU))
        outs["s_k"].append(k.reshape(bs, ss, N_KV_HEADS, HEAD_DIM))
        outs["s_v"].append(v.reshape(bs, ss, N_KV_HEADS, HEAD_DIM))
        outs["s_fc"].append(fc)
    st = {n: jnp.stack(v) for n, v in outs.items()}
    return (xp.reshape(bp, sp, D_MODEL), xs.reshape(bs, ss, D_MODEL),
            st["p_lc"], st["p_lh"], st["p_k"], st["p_v"], st["p_fc"],
            st["s_lc"], st["s_lh"], st["s_k"], st["s_v"], st["s_fc"])
```
